```python
import math
import jax, jax.numpy as jnp
from jax import lax
import numpy as np

D_MODEL = 4096
BATCH = 4
SEQ = 2048
DEPTH = 2
DEC_BATCH = 8
DEC_SEQ = 2048
PAST_LEN = 128

N_MEM = 256
SGU_CHUNK = 128
SGU_GROUPS = 12
SGU_GROUP_DIM = 128
W_A = SGU_GROUPS * SGU_GROUP_DIM
GDN_HEADS = 12
GDN_HEAD_DIM = 128
W_B = GDN_HEADS * GDN_HEAD_DIM
GDN_CHUNK = 64
CONV_K = 5
XA_HEADS = 4
XA_HEAD_DIM = 256
W_C = XA_HEADS * XA_HEAD_DIM
W_MIX = W_A + W_B + W_C
EPS = 1e-6
IN_SIZES = (W_A, W_A, W_A, W_B, W_B, W_B, W_B, GDN_HEADS, GDN_HEADS, GDN_HEADS, GDN_HEADS, W_C, W_C)
N_IN = sum(IN_SIZES)

kernel_name = 'hybrid_sgu_gdn_memxattn_encoder'


def rms_norm(x, g):
    xf = x.astype(jnp.float32)
    y = xf * lax.rsqrt(jnp.mean(xf * xf, axis=-1, keepdims=True) + EPS)
    return (y * g.astype(jnp.float32)).astype(x.dtype)


def layer_norm(x, g, b):
    xf = x.astype(jnp.float32)
    mu = jnp.mean(xf, axis=-1, keepdims=True)
    var = jnp.mean(jnp.square(xf - mu), axis=-1, keepdims=True)
    return ((xf - mu) * lax.rsqrt(var + EPS) * g.astype(jnp.float32) + b.astype(jnp.float32)).astype(x.dtype)


def l2_normalize(x):
    return x * lax.rsqrt(jnp.sum(x * x, axis=-1, keepdims=True) + EPS)


def spatial_gating(u, v, ln_g, ln_b, w_s, b_s):
    B, S, _ = u.shape
    n = S // SGU_CHUNK
    vn = layer_norm(v, ln_g, ln_b).reshape(B, n, SGU_CHUNK, SGU_GROUPS, SGU_GROUP_DIM)
    mixed = jnp.einsum('gts,bnsgc->bntgc', w_s, vn) + b_s.T[None, None, :, :, None]
    return u * mixed.reshape(B, S, W_A)


def centred_depthwise_conv(x, w):
    K, C = w.shape
    return lax.conv_general_dilated(
        x, w[:, None, :].astype(x.dtype), window_strides=(1,),
        padding=[((K - 1) // 2, K // 2)],
        dimension_numbers=('NWC', 'WIO', 'NWC'), feature_group_count=C)


def gated_delta_chunked(q, k, v, g, beta):
    B, S, H, Dk = q.shape
    Dv = v.shape[-1]
    C = GDN_CHUNK
    n = S // C

    def to_chunks(t):
        return jnp.moveaxis(t.reshape(B, n, C, H, -1), 3, 1)

    qc, kc, vc = to_chunks(q), to_chunks(k), to_chunks(v)
    gc = jnp.cumsum(jnp.moveaxis(g.reshape(B, n, C, H), 3, 1), axis=-1)
    bc = jnp.moveaxis(beta.reshape(B, n, C, H), 3, 1)
    lower = jnp.tril(jnp.ones((C, C), dtype=bool))
    strict = jnp.tril(jnp.ones((C, C), dtype=bool), -1)
    diff = gc[..., :, None] - gc[..., None, :]
    decay = jnp.where(lower, jnp.exp(jnp.where(lower, diff, 0.0)), 0.0)
    k_beta = kc * bc[..., None]
    v_beta = vc * bc[..., None]
    a = jnp.where(strict, jnp.einsum('bhnid,bhnjd->bhnij', k_beta, kc) * decay, 0.0)
    eye = jnp.eye(C, dtype=a.dtype)
    t_mat = lax.linalg.triangular_solve(a + eye, jnp.broadcast_to(eye, a.shape), left_side=True, lower=True)
    u = t_mat @ v_beta
    w = t_mat @ (k_beta * jnp.exp(gc)[..., None])
    qk = jnp.einsum('bhnid,bhnjd->bhnij', qc, kc) * decay

    def step(state, xs):
        q_i, k_i, u_i, w_i, g_i, qk_i = xs
        v_new = u_i - w_i @ state
        out = (q_i * jnp.exp(g_i)[..., None]) @ state + qk_i @ v_new
        g_last = g_i[..., -1:]
        state = state * jnp.exp(g_last)[..., None] + jnp.einsum(
            'bhcd,bhce->bhde', k_i * jnp.exp(g_last - g_i)[..., None], v_new)
        return state, out

    xs = tuple(jnp.moveaxis(t, 2, 0) for t in (qc, kc, u, w, gc, qk))
    state0 = jnp.zeros((B, H, Dk, Dv), jnp.float32)
    _, out = lax.scan(step, state0, xs)
    return jnp.transpose(out, (1, 0, 3, 2, 4)).reshape(B, S, H, Dv)


def memory_attention(q, mem, mem_norm_g, w_mem_kv):
    B, S, _ = q.shape
    m = rms_norm(mem, mem_norm_g)
    k, v = jnp.split(m @ w_mem_kv, 2, axis=-1)
    q = q.reshape(B, S, XA_HEADS, XA_HEAD_DIM)
    k = k.reshape(B, -1, XA_HEADS, XA_HEAD_DIM)
    v = v.reshape(B, -1, XA_HEADS, XA_HEAD_DIM)
    s = jnp.einsum('bshd,bmhd->bhsm', q, k).astype(jnp.float32) * (XA_HEAD_DIM ** -0.5)
    p = jax.nn.softmax(s, axis=-1).astype(v.dtype)
    return jnp.einsum('bhsm,bmhd->bshd', p, v).reshape(B, S, W_C)


def encoder_layer(x, mem, norm_g, w_in, sgu_ln_g, sgu_ln_b, sgu_w, sgu_b, conv_w,
                  a_log, dt_bias, gdn_norm_g, mem_norm_g, w_mem_kv, w_out):
    B, S, _ = x.shape
    h = rms_norm(x, norm_g)
    z = h @ w_in
    split_at = np.cumsum(IN_SIZES)[:-1].tolist()
    (u_a, v_a, gate_a, q_b, k_b, v_b, gate_b, beta_fw, beta_bw, dec_fw, dec_bw,
     q_c, gate_c) = jnp.split(z, split_at, axis=-1)

    y_a = spatial_gating(jax.nn.gelu(u_a), jax.nn.gelu(v_a), sgu_ln_g, sgu_ln_b, sgu_w, sgu_b) * jax.nn.silu(gate_a)

    qkv = jax.nn.silu(centred_depthwise_conv(jnp.concatenate([q_b, k_b, v_b], axis=-1), conv_w))
    qkv = qkv.astype(jnp.float32).reshape(B, S, 3, GDN_HEADS, GDN_HEAD_DIM)
    q = l2_normalize(qkv[:, :, 0]) * (GDN_HEAD_DIM ** -0.5)
    k = l2_normalize(qkv[:, :, 1])
    v = qkv[:, :, 2]
    a_log32 = a_log.astype(jnp.float32)
    dt32 = dt_bias.astype(jnp.float32)
    g_fw = -jnp.exp(a_log32[0]) * jax.nn.softplus(dec_fw.astype(jnp.float32) + dt32[0])
    g_bw = -jnp.exp(a_log32[1]) * jax.nn.softplus(dec_bw.astype(jnp.float32) + dt32[1])
    b_fw = jax.nn.sigmoid(beta_fw.astype(jnp.float32))
    b_bw = jax.nn.sigmoid(beta_bw.astype(jnp.float32))
    o_fw = gated_delta_chunked(q, k, v, g_fw, b_fw)
    flip = lambda t: jnp.flip(t, axis=1)
    o_bw = flip(gated_delta_chunked(flip(q), flip(k), flip(v), flip(g_bw), flip(b_bw)))
    o_b = rms_norm(o_fw + o_bw, gdn_norm_g)
    y_b = o_b.reshape(B, S, W_B).astype(x.dtype) * jax.nn.silu(gate_b)

    y_c = memory_attention(q_c, mem, mem_norm_g, w_mem_kv) * jax.nn.silu(gate_c)

    y = jnp.concatenate([y_a, y_b, y_c.astype(x.dtype)], axis=-1) @ w_out
    return x + y


def encoder_trunk(x, mem, norm_g, w_in, sgu_ln_g, sgu_ln_b, sgu_w, sgu_b, conv_w,
                  a_log, dt_bias, gdn_norm_g, mem_norm_g, w_mem_kv, w_out, final_g):
    for l in range(DEPTH):
        x = encoder_layer(x, mem, norm_g[l], w_in[l], sgu_ln_g[l], sgu_ln_b[l], sgu_w[l], sgu_b[l],
                          conv_w[l], a_log[l], dt_bias[l], gdn_norm_g[l], mem_norm_g[l],
                          w_mem_kv[l], w_out[l])
    return rms_norm(x, final_g)


def setup_inputs(seed: int = 0) -> dict:
    key = jax.random.key(seed)
    ks = jax.random.split(key, 20)
    nrm = jax.random.normal
    f32 = jnp.float32
    dt = jnp.exp(jax.random.uniform(ks[11], (DEPTH, 2, GDN_HEADS), f32, math.log(1e-3), math.log(1e-1)))
    return {
        'x_prompt': nrm(ks[0], (BATCH, SEQ, D_MODEL), f32),
        'x_sample': nrm(ks[1], (DEC_BATCH, DEC_SEQ, D_MODEL), f32),
        'mem_prompt': nrm(ks[2], (BATCH, N_MEM, D_MODEL), f32),
        'mem_sample': nrm(ks[3], (DEC_BATCH, N_MEM, D_MODEL), f32),
        'norm_g': 1.0 + 0.02 * nrm(ks[4], (DEPTH, D_MODEL), f32),
        'w_in': nrm(ks[5], (DEPTH, D_MODEL, N_IN), f32) * D_MODEL ** -0.5,
        'sgu_ln_g': 1.0 + 0.02 * nrm(ks[6], (DEPTH, W_A), f32),
        'sgu_ln_b': 0.02 * nrm(ks[7], (DEPTH, W_A), f32),
        'sgu_w': nrm(ks[8], (DEPTH, SGU_GROUPS, SGU_CHUNK, SGU_CHUNK), f32) * (0.5 * SGU_CHUNK ** -0.5),
        'sgu_b': 1.0 + 0.02 * nrm(ks[9], (DEPTH, SGU_GROUPS, SGU_CHUNK), f32),
        'conv_w': nrm(ks[10], (DEPTH, CONV_K, 3 * W_B), f32) * CONV_K ** -0.5,
        'a_log': jnp.log(jax.random.uniform(ks[12], (DEPTH, 2, GDN_HEADS), f32, 1.0, 16.0)),
        'dt_bias': dt + jnp.log(-jnp.expm1(-dt)),
        'gdn_norm_g': 1.0 + 0.02 * nrm(ks[13], (DEPTH, GDN_HEAD_DIM), f32),
        'mem_norm_g': 1.0 + 0.02 * nrm(ks[14], (DEPTH, D_MODEL), f32),
        'w_mem_kv': nrm(ks[15], (DEPTH, D_MODEL, 2 * W_C), f32) * D_MODEL ** -0.5,
        'w_out': nrm(ks[16], (DEPTH, W_MIX, D_MODEL), f32) * W_MIX ** -0.5,
        'final_g': 1.0 + 0.02 * nrm(ks[17], (D_MODEL,), f32),
    }


def reference(x_prompt, x_sample, mem_prompt, mem_sample, norm_g, w_in, sgu_ln_g, sgu_ln_b,
              sgu_w, sgu_b, conv_w, a_log, dt_bias, gdn_norm_g, mem_norm_g, w_mem_kv, w_out, final_g):
    y_prompt = encoder_trunk(x_prompt, mem_prompt, norm_g, w_in, sgu_ln_g, sgu_ln_b, sgu_w, sgu_b,
                             conv_w, a_log, dt_bias, gdn_norm_g, mem_norm_g, w_mem_kv, w_out, final_g)
    y_sample = encoder_trunk(x_sample, mem_sample, norm_g, w_in, sgu_ln_g, sgu_ln_b, sgu_w, sgu_b,
                             conv_w, a_log, dt_bias, gdn_norm_g, mem_norm_g, w_mem_kv, w_out, final_g)
    return (y_prompt, y_sample)
```

```python
import functools

import jax
import jax.numpy as jnp
from jax import lax
from jax.experimental import pallas as pl
from jax.experimental.pallas import tpu as pltpu

F32 = jnp.float32
BF16 = jnp.bfloat16
EPS = 1e-6

SGU_CHUNK = 128
SGU_GROUPS = 12
GDN_HEADS = 12
HEAD_DIM = 128
CONV_K = 5
XA_HEADS = 4
XA_HEAD_DIM = 256
W_A = SGU_GROUPS * SGU_CHUNK
W_B = GDN_HEADS * HEAD_DIM
W_C = XA_HEADS * XA_HEAD_DIM
W_MAIN = 3 * W_A + 4 * W_B + 2 * W_C
N_SMALL = 4 * GDN_HEADS
OFF_SMALL = 3 * W_A + 4 * W_B

LANE = 128
SUBLANE = 8
VMEM_LIMIT = 56 * 1024 * 1024

GDN_CHUNK = 128
G_BETA, G_CUM, G_TOT = 0, 2 * GDN_HEADS, 4 * GDN_HEADS


def _params(*sem):
    return pltpu.CompilerParams(dimension_semantics=sem, vmem_limit_bytes=VMEM_LIMIT)


def _bdot(a, b):
    return jnp.dot(a.astype(BF16), b.astype(BF16), preferred_element_type=F32)


def _norm_matmul_body(*refs, has_small):
    if has_small:
        x_ref, g_ref, w_ref, ws_ref, z_ref, zs_ref, h_ref = refs
    else:
        x_ref, g_ref, w_ref, z_ref, h_ref = refs

    @pl.when(pl.program_id(1) == 0)
    def _():
        x = x_ref[...]
        ms = jnp.mean(x * x, axis=-1, keepdims=True)
        h = (x * lax.rsqrt(ms + EPS) * g_ref[...]).astype(BF16)
        h_ref[...] = h
        if has_small:
            zs_ref[...] = jnp.dot(h, ws_ref[...], preferred_element_type=F32)

    z_ref[...] = jnp.dot(h_ref[...], w_ref[...], preferred_element_type=F32).astype(z_ref.dtype)


def _norm_matmul(x, g, w, w_small=None, *, tm, tn, out_dtype):
    t, d = x.shape
    n = w.shape[1]
    tm = min(tm, t)
    has_small = w_small is not None
    in_specs = [
        pl.BlockSpec((tm, d), lambda i, j: (i, 0)),
        pl.BlockSpec((1, d), lambda i, j: (0, 0)),
        pl.BlockSpec((d, tn), lambda i, j: (0, j)),
    ]
    out_shape = [jax.ShapeDtypeStruct((t, n), out_dtype)]
    out_specs = [pl.BlockSpec((tm, tn), lambda i, j: (i, j))]
    args = [x, g.reshape(1, d), w]
    if has_small:
        in_specs.append(pl.BlockSpec((d, LANE), lambda i, j: (0, 0)))
        out_shape.append(jax.ShapeDtypeStruct((t, LANE), F32))
        out_specs.append(pl.BlockSpec((tm, LANE), lambda i, j: (i, 0)))
        args.append(w_small)
    outs = pl.pallas_call(
        functools.partial(_norm_matmul_body, has_small=has_small),
        grid=(t // tm, n // tn),
        in_specs=in_specs,
        out_specs=out_specs,
        out_shape=out_shape,
        scratch_shapes=[pltpu.VMEM((tm, d), BF16)],
        compiler_params=_params("parallel", "arbitrary"),
        name="norm_matmul_small" if has_small else "norm_matmul",
    )(*args)
    return outs if has_small else outs[0]


def _gate_prep_body(zs_ref, a_ref, dt_ref, g_ref, gt_ref, *, chunks):
    c = GDN_CHUNK
    lane = lax.broadcasted_iota(jnp.int32, (c, LANE), 1)
    ri = lax.broadcasted_iota(jnp.int32, (c, c), 0)
    ci = lax.broadcasted_iota(jnp.int32, (c, c), 1)
    lower = (ri >= ci).astype(F32)
    upper = (ri <= ci).astype(F32)
    is_decay = (lane >= G_CUM) & (lane < G_TOT)
    neg_a = -jnp.exp(a_ref[...])
    for n in range(chunks):
        zs = zs_ref[pl.ds(n * c, c), :]
        beta = jax.nn.sigmoid(zs)
        g = jnp.where(is_decay, neg_a * jax.nn.softplus(zs + dt_ref[...]), 0.0)
        cum_fw = jnp.dot(lower, g, preferred_element_type=F32, precision=lax.Precision.HIGHEST)
        cum_bw = jnp.dot(upper, g, preferred_element_type=F32, precision=lax.Precision.HIGHEST)
        tot = jnp.broadcast_to(jnp.sum(g, axis=0, keepdims=True), (c, LANE))
        tot = pltpu.roll(tot, G_TOT - G_CUM, 1)
        out = jnp.where(lane < G_CUM, beta,
                        jnp.where(lane < G_CUM + GDN_HEADS, cum_fw,
                                  jnp.where(lane < G_TOT, cum_bw,
                                            jnp.where(lane < G_TOT + 2 * GDN_HEADS, tot, 0.0))))
        g_ref[pl.ds(n * c, c), :] = out
        gt_ref[n] = out.T


def _gate_prep(zs, a_row, dt_row, *, rows):
    t = zs.shape[0]
    rows = min(rows, t)
    chunks = rows // GDN_CHUNK
    return pl.pallas_call(
        functools.partial(_gate_prep_body, chunks=chunks),
        grid=(t // rows,),
        in_specs=[
            pl.BlockSpec((rows, LANE), lambda i: (i, 0)),
            pl.BlockSpec((1, LANE), lambda i: (0, 0)),
            pl.BlockSpec((1, LANE), lambda i: (0, 0)),
        ],
        out_specs=[
            pl.BlockSpec((rows, LANE), lambda i: (i, 0)),
            pl.BlockSpec((chunks, LANE, GDN_CHUNK), lambda i: (i, 0, 0)),
        ],
        out_shape=[
            jax.ShapeDtypeStruct((t, LANE), F32),
            jax.ShapeDtypeStruct((t // GDN_CHUNK, LANE, GDN_CHUNK), F32),
        ],
        compiler_params=_params("parallel"),
        name="gate_prep",
    )(zs, a_row, dt_row)


def _sgu_body(u_ref, v_ref, gate_ref, lg_ref, lb_ref, ws_ref, bs_ref, y_ref, *, chunks):
    c = SGU_CHUNK
    v = jax.nn.gelu(v_ref[0])
    mu = jnp.mean(v, axis=-1, keepdims=True)
    vc = v - mu
    var = jnp.mean(vc * vc, axis=-1, keepdims=True)
    vn = (vc * lax.rsqrt(var + EPS) * lg_ref[...] + lb_ref[...]).astype(BF16)
    for n in range(chunks):
        for g in range(SGU_GROUPS):
            rows, cols = slice(n * c, (n + 1) * c), slice(g * c, (g + 1) * c)
            mixed = jnp.dot(ws_ref[g], vn[rows, cols], preferred_element_type=F32) + bs_ref[g]
            gate = gate_ref[0, rows, cols]
            y = jax.nn.gelu(u_ref[0, rows, cols]) * mixed * (gate * jax.nn.sigmoid(gate))
            y_ref[0, rows, cols] = y.astype(y_ref.dtype)


def _sgu(z3, ln_g, ln_b, w_s, b_s, *, rows):
    b, s, _ = z3.shape
    rows = min(rows, s)
    chunks = rows // SGU_CHUNK
    col = lambda k: pl.BlockSpec((1, rows, W_A), lambda i, r, k=k: (i, r, k))
    return pl.pallas_call(
        functools.partial(_sgu_body, chunks=chunks),
        grid=(b, s // rows),
        in_specs=[
            col(0), col(1), col(2),
            pl.BlockSpec((1, W_A), lambda i, r: (0, 0)),
            pl.BlockSpec((1, W_A), lambda i, r: (0, 0)),
            pl.BlockSpec((SGU_GROUPS, SGU_CHUNK, SGU_CHUNK), lambda i, r: (0, 0, 0)),
            pl.BlockSpec((SGU_GROUPS, SGU_CHUNK, LANE), lambda i, r: (0, 0, 0)),
        ],
        out_specs=pl.BlockSpec((1, rows, W_A), lambda i, r: (i, r, 0)),
        out_shape=jax.ShapeDtypeStruct((b, s, W_A), BF16),
        compiler_params=_params("parallel", "parallel"),
        name="sgu",
    )(z3, z3, z3, ln_g.reshape(1, W_A), ln_b.reshape(1, W_A), w_s.astype(BF16),
      jnp.broadcast_to(b_s[:, :, None], (SGU_GROUPS, SGU_CHUNK, LANE)))


def _xattn_body(q_ref, gate_ref, k_ref, v_ref, y_ref):
    q = q_ref[0].astype(BF16)
    s = lax.dot_general(q, k_ref[0], (((1,), (1,)), ((), ())), preferred_element_type=F32)
    s = s * (XA_HEAD_DIM ** -0.5)
    s = s - jnp.max(s, axis=-1, keepdims=True)
    p = jnp.exp(s)
    p = p / jnp.sum(p, axis=-1, keepdims=True)
    o = jnp.dot(p.astype(BF16), v_ref[0], preferred_element_type=F32)
    gate = gate_ref[0]
    y_ref[0] = (o * (gate * jax.nn.sigmoid(gate))).astype(y_ref.dtype)


def _xattn(z3, kv3, *, rows):
    b, s, _ = z3.shape
    rows = min(rows, s)
    n_mem = kv3.shape[1]
    q_blk = (3 * W_A + 4 * W_B) // XA_HEAD_DIM
    gate_blk = q_blk + XA_HEADS
    return pl.pallas_call(
        _xattn_body,
        grid=(b, s // rows, XA_HEADS),
        in_specs=[
            pl.BlockSpec((1, rows, XA_HEAD_DIM), lambda i, r, h: (i, r, q_blk + h)),
            pl.BlockSpec((1, rows, XA_HEAD_DIM), lambda i, r, h: (i, r, gate_blk + h)),
            pl.BlockSpec((1, n_mem, XA_HEAD_DIM), lambda i, r, h: (i, 0, h)),
            pl.BlockSpec((1, n_mem, XA_HEAD_DIM), lambda i, r, h: (i, 0, XA_HEADS + h)),
        ],
        out_specs=pl.BlockSpec((1, rows, XA_HEAD_DIM), lambda i, r, h: (i, r, h)),
        out_shape=jax.ShapeDtypeStruct((b, s, W_C), BF16),
        compiler_params=_params("parallel", "parallel", "parallel"),
        name="xattn",
    )(z3, z3, kv3, kv3)


def _unit_triangular_inverse(a, xor_idx):
    c = a.shape[0]
    eye = (xor_idx == 0).astype(F32)
    n1 = jnp.where(xor_idx < 16, -a, 0.0)
    n2 = _bdot(n1, n1)
    p = eye + n1
    p = p + _bdot(p, n2)
    n4 = _bdot(n2, n2)
    p = p + _bdot(p, n4)
    n8 = _bdot(n4, n4)
    t = p + _bdot(p, n8)
    for shift in (4, 5, 6):
        off = jnp.where((xor_idx >> shift) == 1, a, 0.0)
        t = t - _bdot(t, _bdot(off, t))
    del c
    return t


def _gdn_body(q_ref, k_ref, v_ref, gate_ref, g_ref, gt_ref, cq_ref, ck_ref, cv_ref, ng_ref, y_ref,
              u_s, wq_s, qk_s, kg_s, eg_s, o_s, *, seq):
    c = GDN_CHUNK
    n_chunks = seq // c
    h = pl.program_id(1)
    lane = lax.broadcasted_iota(jnp.int32, (c, LANE), 1)
    sub8 = lax.broadcasted_iota(jnp.int32, (SUBLANE, c), 0)
    ri = lax.broadcasted_iota(jnp.int32, (c, c), 0)
    ci = lax.broadcasted_iota(jnp.int32, (c, c), 1)
    xor_idx = ri ^ ci

    def conv_silu(x_ref, cw_ref, ci_, c0):
        halo = SUBLANE
        prev0 = pl.multiple_of(jnp.maximum(c0 - halo, 0), SUBLANE)
        next0 = pl.multiple_of(jnp.minimum(c0 + c, seq - halo), SUBLANE)
        prev = jnp.where(ci_ > 0, x_ref[0, pl.ds(prev0, halo), :], 0.0)
        nxt = jnp.where(ci_ < n_chunks - 1, x_ref[0, pl.ds(next0, halo), :], 0.0)
        win = jnp.concatenate([prev, x_ref[0, pl.ds(c0, c), :], nxt], axis=0)
        cw = cw_ref[...]
        acc = None
        for j in range(CONV_K):
            start = halo + j - (CONV_K - 1) // 2
            term = win[start:start + c, :] * cw[j:j + 1, :]
            acc = term if acc is None else acc + term
        return acc * jax.nn.sigmoid(acc)

    def l2n(x):
        return x * lax.rsqrt(jnp.sum(x * x, axis=-1, keepdims=True) + EPS)

    def column(gc, idx):
        return jnp.sum(jnp.where(lane == idx, gc, 0.0), axis=1, keepdims=True)

    def gt_row(ci_, idx):
        base = pl.multiple_of((idx >> 3) << 3, SUBLANE)
        rows = gt_ref[0, ci_, pl.ds(base, SUBLANE), :]
        return jnp.sum(jnp.where(sub8 == (idx & 7), rows, 0.0), axis=0, keepdims=True)

    def prep(ci_, carry):
        c0 = pl.multiple_of(ci_ * c, c)
        q = l2n(conv_silu(q_ref, cq_ref, ci_, c0)) * (HEAD_DIM ** -0.5)
        k = l2n(conv_silu(k_ref, ck_ref, ci_, c0))
        v = conv_silu(v_ref, cv_ref, ci_, c0)
        kb = k.astype(BF16)
        qkk = lax.dot_general(jnp.concatenate([q.astype(BF16), kb], axis=0), kb,
                              (((1,), (1,)), ((), ())), preferred_element_type=F32)
        qk, kk = qkk[:c], qkk[c:]
        gc = g_ref[0, pl.ds(c0, c), :]
        for d in range(2):
            beta = column(gc, G_BETA + GDN_HEADS * d + h)
            cum_c = column(gc, G_CUM + GDN_HEADS * d + h)
            cum_r = gt_row(ci_, G_CUM + GDN_HEADS * d + h)
            tot_r = gt_row(ci_, G_TOT + GDN_HEADS * d + h)
            incl = (ri >= ci) if d == 0 else (ri <= ci)
            strict = (ri > ci) if d == 0 else (ri < ci)
            decay = jnp.where(incl, jnp.exp(jnp.where(incl, cum_c - cum_r, 0.0)), 0.0)
            a = jnp.where(strict, kk * decay * beta, 0.0)
            t = _unit_triangular_inverse(a, xor_idx)
            e_c = jnp.exp(cum_c)
            rhs = jnp.concatenate([v * beta, k * (beta * e_c)], axis=1)
            uw = _bdot(t, rhs)
            u_s[d, ci_] = uw[:, :HEAD_DIM]
            wq_s[d, ci_] = jnp.concatenate([uw[:, HEAD_DIM:], q * e_c], axis=0).astype(BF16)
            qk_s[d, ci_] = (qk * decay).astype(BF16)
            kg_s[d, ci_] = (k * jnp.exp(tot_r - cum_c)).astype(BF16)
            eg_s[d, ci_] = jnp.broadcast_to(jnp.exp(tot_r), (SUBLANE, c))
        return carry

    lax.fori_loop(0, n_chunks, prep, 0)

    def scan(i, states):
        new = []
        for d in range(2):
            ci_ = i if d == 0 else n_chunks - 1 - i
            sb = states[d].astype(BF16)
            ws = jnp.dot(wq_s[d, ci_], sb, preferred_element_type=F32)
            v_new = (u_s[d, ci_] - ws[:c]).astype(BF16)
            o = ws[c:] + jnp.dot(qk_s[d, ci_], v_new, preferred_element_type=F32)
            o_s[d, pl.ds(pl.multiple_of(ci_ * c, c), c), :] = o
            upd = lax.dot_general(kg_s[d, ci_], v_new, (((0,), (0,)), ((), ())),
                                  preferred_element_type=F32)
            new.append(states[d] * eg_s[d, ci_][0:1, :] + upd)
        return tuple(new)

    zero = jnp.zeros((HEAD_DIM, HEAD_DIM), F32)
    lax.fori_loop(0, n_chunks, scan, (zero, zero))

    o = o_s[0] + o_s[1]
    o = o * lax.rsqrt(jnp.mean(o * o, axis=-1, keepdims=True) + EPS) * ng_ref[...]
    gate = gate_ref[0]
    y_ref[0] = (o * (gate * jax.nn.sigmoid(gate))).astype(y_ref.dtype)


def _gdn(z3, g3, gt4, conv_w, norm_g):
    b, s, _ = z3.shape
    n_chunks = s // GDN_CHUNK
    blk0 = 3 * W_A // HEAD_DIM
    head = lambda k: pl.BlockSpec((1, s, HEAD_DIM), lambda i, h, k=k: (i, 0, blk0 + k * GDN_HEADS + h))
    cw = lambda k: pl.BlockSpec((CONV_K, HEAD_DIM), lambda i, h, k=k: (0, k * GDN_HEADS + h))
    return pl.pallas_call(
        functools.partial(_gdn_body, seq=s),
        grid=(b, GDN_HEADS),
        in_specs=[
            head(0), head(1), head(2), head(3),
            pl.BlockSpec((1, s, LANE), lambda i, h: (i, 0, 0)),
            pl.BlockSpec((1, n_chunks, LANE, GDN_CHUNK), lambda i, h: (i, 0, 0, 0)),
            cw(0), cw(1), cw(2),
            pl.BlockSpec((1, HEAD_DIM), lambda i, h: (0, 0)),
        ],
        out_specs=pl.BlockSpec((1, s, HEAD_DIM), lambda i, h: (i, 0, h)),
        out_shape=jax.ShapeDtypeStruct((b, s, W_B), BF16),
        scratch_shapes=[
            pltpu.VMEM((2, n_chunks, GDN_CHUNK, HEAD_DIM), F32),
            pltpu.VMEM((2, n_chunks, 2 * GDN_CHUNK, HEAD_DIM), BF16),
            pltpu.VMEM((2, n_chunks, GDN_CHUNK, GDN_CHUNK), BF16),
            pltpu.VMEM((2, n_chunks, GDN_CHUNK, HEAD_DIM), BF16),
            pltpu.VMEM((2, n_chunks, SUBLANE, HEAD_DIM), F32),
            pltpu.VMEM((2, s, HEAD_DIM), F32),
        ],
        compiler_params=_params("parallel", "arbitrary"),
        name="gdn",
    )(z3, z3, z3, z3, g3, gt4, conv_w, conv_w, conv_w, norm_g.reshape(1, HEAD_DIM))


def _out_proj_body(ya_ref, yb_ref, yc_ref, wa_ref, wb_ref, wc_ref, x_ref, o_ref):
    acc = jnp.dot(ya_ref[...], wa_ref[...], preferred_element_type=F32)
    acc += jnp.dot(yb_ref[...], wb_ref[...], preferred_element_type=F32)
    acc += jnp.dot(yc_ref[...], wc_ref[...], preferred_element_type=F32)
    o_ref[...] = x_ref[...] + acc


def _out_proj(ya, yb, yc, w_out, x, *, tm, tn):
    t, d = x.shape
    tm = min(tm, t)
    wa, wb, wc = w_out[:W_A], w_out[W_A:W_A + W_B], w_out[W_A + W_B:]
    row = lambda w: pl.BlockSpec((tm, w), lambda i, j: (i, 0))
    wcol = lambda w: pl.BlockSpec((w, tn), lambda i, j: (0, j))
    return pl.pallas_call(
        _out_proj_body,
        grid=(t // tm, d // tn),
        in_specs=[row(W_A), row(W_B), row(W_C), wcol(W_A), wcol(W_B), wcol(W_C),
                  pl.BlockSpec((tm, tn), lambda i, j: (i, j))],
        out_specs=pl.BlockSpec((tm, tn), lambda i, j: (i, j)),
        out_shape=jax.ShapeDtypeStruct((t, d), F32),
        compiler_params=_params("parallel", "parallel"),
        name="out_proj",
    )(ya, yb, yc, wa, wb, wc, x)


def _rms_body(x_ref, g_ref, o_ref):
    x = x_ref[...]
    o_ref[...] = x * lax.rsqrt(jnp.mean(x * x, axis=-1, keepdims=True) + EPS) * g_ref[...]


def _rms_norm(x, g, *, tm):
    t, d = x.shape
    tm = min(tm, t)
    return pl.pallas_call(
        _rms_body,
        grid=(t // tm,),
        in_specs=[pl.BlockSpec((tm, d), lambda i: (i, 0)), pl.BlockSpec((1, d), lambda i: (0, 0))],
        out_specs=pl.BlockSpec((tm, d), lambda i: (i, 0)),
        out_shape=jax.ShapeDtypeStruct((t, d), F32),
        compiler_params=_params("parallel"),
        name="final_norm",
    )(x, g.reshape(1, d))


def _lane_row(values, offset):
    flat = values.reshape(-1).astype(F32)
    return jnp.zeros((1, LANE), F32).at[0, offset:offset + flat.shape[0]].set(flat)


def _layer(x, mem, norm_g, w_in, sgu_ln_g, sgu_ln_b, sgu_w, sgu_b, conv_w, a_log, dt_bias,
           gdn_norm_g, mem_norm_g, w_mem_kv, w_out):
    b, s, d = x.shape
    t = b * s
    x2 = x.reshape(t, d)
    w_main = jnp.concatenate([w_in[:, :OFF_SMALL], w_in[:, OFF_SMALL + N_SMALL:]], axis=1).astype(BF16)
    w_small = jnp.pad(w_in[:, OFF_SMALL:OFF_SMALL + N_SMALL], ((0, 0), (0, LANE - N_SMALL))).astype(BF16)
    z, zs = _norm_matmul(x2, norm_g, w_main, w_small, tm=512, tn=1280, out_dtype=F32)
    z3 = z.reshape(b, s, W_MAIN)

    g, gt = _gate_prep(zs, _lane_row(a_log, G_CUM), _lane_row(dt_bias, G_CUM), rows=512)
    g3 = g.reshape(b, s, LANE)
    gt4 = gt.reshape(b, s // GDN_CHUNK, LANE, GDN_CHUNK)

    ya = _sgu(z3, sgu_ln_g, sgu_ln_b, sgu_w, sgu_b, rows=512)
    yb = _gdn(z3, g3, gt4, conv_w, gdn_norm_g)
    kv = _norm_matmul(mem.reshape(-1, d), mem_norm_g, w_mem_kv.astype(BF16), tm=512, tn=1024,
                      out_dtype=BF16)
    yc = _xattn(z3, kv.reshape(b, -1, 2 * W_C), rows=512)

    out = _out_proj(ya.reshape(t, W_A), yb.reshape(t, W_B), yc.reshape(t, W_C),
                    w_out.astype(BF16), x2, tm=1024, tn=512)
    return out.reshape(b, s, d)


def kernel(x_prompt, x_sample, mem_prompt, mem_sample, norm_g, w_in, sgu_ln_g, sgu_ln_b, sgu_w,
           sgu_b, conv_w, a_log, dt_bias, gdn_norm_g, mem_norm_g, w_mem_kv, w_out, final_g):
    n_prompt = x_prompt.shape[0]
    x = jnp.concatenate([x_prompt, x_sample], axis=0)
    mem = jnp.concatenate([mem_prompt, mem_sample], axis=0)
    for l in range(norm_g.shape[0]):
        x = _layer(x, mem, norm_g[l], w_in[l], sgu_ln_g[l], sgu_ln_b[l], sgu_w[l], sgu_b[l],
                   conv_w[l], a_log[l], dt_bias[l], gdn_norm_g[l], mem_norm_g[l], w_mem_kv[l],
                   w_out[l])
    b, s, d = x.shape
    y = _rms_norm(x.reshape(b * s, d), final_g, tm=512).reshape(b, s, d)
    return y[:n_prompt], y[n_prompt:]
```

```python
import functools

import jax
import jax.numpy as jnp
import numpy as np
from jax import lax
from jax.experimental import pallas as pl
from jax.experimental.pallas import tpu as pltpu

F32 = jnp.float32
BF16 = jnp.bfloat16
EPS = 1e-6

SGU_CHUNK = 128
SGU_GROUPS = 12
GDN_HEADS = 12
HEAD_DIM = 128
CONV_K = 5
XA_HEADS = 4
XA_HEAD_DIM = 256
W_A = SGU_GROUPS * SGU_CHUNK
W_B = GDN_HEADS * HEAD_DIM
W_C = XA_HEADS * XA_HEAD_DIM
W_MAIN = 3 * W_A + 4 * W_B + 2 * W_C
N_SMALL = 4 * GDN_HEADS
OFF_SMALL = 3 * W_A + 4 * W_B

LANE = 128
SUBLANE = 8
VMEM_LIMIT = 56 * 1024 * 1024

GDN_CHUNK = 128
GDN_GROUP = 8
G_BETA, G_CUM, G_TOT = 0, 2 * GDN_HEADS, 4 * GDN_HEADS
M_INCL, M_EYE, M_NEG_DIAG16, M_OFF16, M_OFF32, M_OFF64 = range(6)


def _params(*sem):
    return pltpu.CompilerParams(dimension_semantics=sem, vmem_limit_bytes=VMEM_LIMIT)


def _bdot(a, b):
    return jnp.dot(a.astype(BF16), b.astype(BF16), preferred_element_type=F32)


def _norm_matmul_body(*refs, has_small):
    if has_small:
        x_ref, g_ref, w_ref, ws_ref, z_ref, zs_ref, h_ref = refs
    else:
        x_ref, g_ref, w_ref, z_ref, h_ref = refs

    @pl.when(pl.program_id(1) == 0)
    def _():
        x = x_ref[...]
        ms = jnp.mean(x * x, axis=-1, keepdims=True)
        h = (x * lax.rsqrt(ms + EPS) * g_ref[...]).astype(BF16)
        h_ref[...] = h
        if has_small:
            zs_ref[...] = jnp.dot(h, ws_ref[...], preferred_element_type=F32)

    z_ref[...] = jnp.dot(h_ref[...], w_ref[...], preferred_element_type=F32).astype(z_ref.dtype)


def _norm_matmul(x, g, w, w_small=None, *, tm, tn, out_dtype):
    t, d = x.shape
    n = w.shape[1]
    tm = min(tm, t)
    has_small = w_small is not None
    in_specs = [
        pl.BlockSpec((tm, d), lambda i, j: (i, 0)),
        pl.BlockSpec((1, d), lambda i, j: (0, 0)),
        pl.BlockSpec((d, tn), lambda i, j: (0, j)),
    ]
    out_shape = [jax.ShapeDtypeStruct((t, n), out_dtype)]
    out_specs = [pl.BlockSpec((tm, tn), lambda i, j: (i, j))]
    args = [x, g.reshape(1, d), w]
    if has_small:
        in_specs.append(pl.BlockSpec((d, LANE), lambda i, j: (0, 0)))
        out_shape.append(jax.ShapeDtypeStruct((t, LANE), F32))
        out_specs.append(pl.BlockSpec((tm, LANE), lambda i, j: (i, 0)))
        args.append(w_small)
    outs = pl.pallas_call(
        functools.partial(_norm_matmul_body, has_small=has_small),
        grid=(t // tm, n // tn),
        in_specs=in_specs,
        out_specs=out_specs,
        out_shape=out_shape,
        scratch_shapes=[pltpu.VMEM((tm, d), BF16)],
        compiler_params=_params("parallel", "arbitrary"),
        name="norm_matmul_small" if has_small else "norm_matmul",
    )(*args)
    return outs if has_small else outs[0]


def _gate_prep_body(zs_ref, a_ref, dt_ref, g_ref, gt_ref, *, chunks):
    c = GDN_CHUNK
    lane = lax.broadcasted_iota(jnp.int32, (c, LANE), 1)
    ri = lax.broadcasted_iota(jnp.int32, (c, c), 0)
    ci = lax.broadcasted_iota(jnp.int32, (c, c), 1)
    lower = (ri >= ci).astype(F32)
    upper = (ri <= ci).astype(F32)
    is_decay = (lane >= G_CUM) & (lane < G_TOT)
    neg_a = -jnp.exp(a_ref[...])
    for n in range(chunks):
        zs = zs_ref[pl.ds(n * c, c), :]
        beta = jax.nn.sigmoid(zs)
        g = jnp.where(is_decay, neg_a * jax.nn.softplus(zs + dt_ref[...]), 0.0)
        cum_fw = jnp.dot(lower, g, preferred_element_type=F32, precision=lax.Precision.HIGHEST)
        cum_bw = jnp.dot(upper, g, preferred_element_type=F32, precision=lax.Precision.HIGHEST)
        tot = jnp.broadcast_to(jnp.sum(g, axis=0, keepdims=True), (c, LANE))
        tot = pltpu.roll(tot, G_TOT - G_CUM, 1)
        out = jnp.where(lane < G_CUM, beta,
                        jnp.where(lane < G_CUM + GDN_HEADS, cum_fw,
                                  jnp.where(lane < G_TOT, cum_bw,
                                            jnp.where(lane < G_TOT + 2 * GDN_HEADS, tot, 0.0))))
        g_ref[pl.ds(n * c, c), :] = out
        gt_ref[n] = out.T


def _gate_prep(zs, a_row, dt_row, *, rows):
    t = zs.shape[0]
    rows = min(rows, t)
    chunks = rows // GDN_CHUNK
    return pl.pallas_call(
        functools.partial(_gate_prep_body, chunks=chunks),
        grid=(t // rows,),
        in_specs=[
            pl.BlockSpec((rows, LANE), lambda i: (i, 0)),
            pl.BlockSpec((1, LANE), lambda i: (0, 0)),
            pl.BlockSpec((1, LANE), lambda i: (0, 0)),
        ],
        out_specs=[
            pl.BlockSpec((rows, LANE), lambda i: (i, 0)),
            pl.BlockSpec((chunks, LANE, GDN_CHUNK), lambda i: (i, 0, 0)),
        ],
        out_shape=[
            jax.ShapeDtypeStruct((t, LANE), F32),
            jax.ShapeDtypeStruct((t // GDN_CHUNK, LANE, GDN_CHUNK), F32),
        ],
        compiler_params=_params("parallel"),
        name="gate_prep",
    )(zs, a_row, dt_row)


def _sgu_body(u_ref, v_ref, gate_ref, lg_ref, lb_ref, ws_ref, bs_ref, y_ref, *, chunks):
    c = SGU_CHUNK
    v = jax.nn.gelu(v_ref[0])
    mu = jnp.mean(v, axis=-1, keepdims=True)
    vc = v - mu
    var = jnp.mean(vc * vc, axis=-1, keepdims=True)
    vn = (vc * lax.rsqrt(var + EPS) * lg_ref[...] + lb_ref[...]).astype(BF16)
    for n in range(chunks):
        for g in range(SGU_GROUPS):
            rows, cols = slice(n * c, (n + 1) * c), slice(g * c, (g + 1) * c)
            mixed = jnp.dot(ws_ref[g], vn[rows, cols], preferred_element_type=F32) + bs_ref[g]
            gate = gate_ref[0, rows, cols]
            y = jax.nn.gelu(u_ref[0, rows, cols]) * mixed * (gate * jax.nn.sigmoid(gate))
            y_ref[0, rows, cols] = y.astype(y_ref.dtype)


def _sgu(z3, ln_g, ln_b, w_s, b_s, *, rows):
    b, s, _ = z3.shape
    rows = min(rows, s)
    chunks = rows // SGU_CHUNK
    col = lambda k: pl.BlockSpec((1, rows, W_A), lambda i, r, k=k: (i, r, k))
    return pl.pallas_call(
        functools.partial(_sgu_body, chunks=chunks),
        grid=(b, s // rows),
        in_specs=[
            col(0), col(1), col(2),
            pl.BlockSpec((1, W_A), lambda i, r: (0, 0)),
            pl.BlockSpec((1, W_A), lambda i, r: (0, 0)),
            pl.BlockSpec((SGU_GROUPS, SGU_CHUNK, SGU_CHUNK), lambda i, r: (0, 0, 0)),
            pl.BlockSpec((SGU_GROUPS, SGU_CHUNK, LANE), lambda i, r: (0, 0, 0)),
        ],
        out_specs=pl.BlockSpec((1, rows, W_A), lambda i, r: (i, r, 0)),
        out_shape=jax.ShapeDtypeStruct((b, s, W_A), BF16),
        compiler_params=_params("parallel", "parallel"),
        name="sgu",
    )(z3, z3, z3, ln_g.reshape(1, W_A), ln_b.reshape(1, W_A), w_s.astype(BF16),
      jnp.broadcast_to(b_s[:, :, None], (SGU_GROUPS, SGU_CHUNK, LANE)))


def _xattn_body(q_ref, gate_ref, k_ref, v_ref, y_ref):
    q = q_ref[0].astype(BF16)
    s = lax.dot_general(q, k_ref[0], (((1,), (1,)), ((), ())), preferred_element_type=F32)
    s = s * (XA_HEAD_DIM ** -0.5)
    s = s - jnp.max(s, axis=-1, keepdims=True)
    p = jnp.exp(s)
    p = p / jnp.sum(p, axis=-1, keepdims=True)
    o = jnp.dot(p.astype(BF16), v_ref[0], preferred_element_type=F32)
    gate = gate_ref[0]
    y_ref[0] = (o * (gate * jax.nn.sigmoid(gate))).astype(y_ref.dtype)


def _xattn(z3, kv3, *, rows):
    b, s, _ = z3.shape
    rows = min(rows, s)
    n_mem = kv3.shape[1]
    q_blk = (3 * W_A + 4 * W_B) // XA_HEAD_DIM
    gate_blk = q_blk + XA_HEADS
    return pl.pallas_call(
        _xattn_body,
        grid=(b, s // rows, XA_HEADS),
        in_specs=[
            pl.BlockSpec((1, rows, XA_HEAD_DIM), lambda i, r, h: (i, r, q_blk + h)),
            pl.BlockSpec((1, rows, XA_HEAD_DIM), lambda i, r, h: (i, r, gate_blk + h)),
            pl.BlockSpec((1, n_mem, XA_HEAD_DIM), lambda i, r, h: (i, 0, h)),
            pl.BlockSpec((1, n_mem, XA_HEAD_DIM), lambda i, r, h: (i, 0, XA_HEADS + h)),
        ],
        out_specs=pl.BlockSpec((1, rows, XA_HEAD_DIM), lambda i, r, h: (i, r, h)),
        out_shape=jax.ShapeDtypeStruct((b, s, W_C), BF16),
        compiler_params=_params("parallel", "parallel", "parallel"),
        name="xattn",
    )(z3, z3, kv3, kv3)


def _gdn_masks():
    c = GDN_CHUNK
    ri = np.arange(c)[:, None]
    ci = np.arange(c)[None, :]
    x = ri ^ ci
    halves = []
    for strict, incl in ((ri > ci, ri >= ci), (ri < ci, ri <= ci)):
        halves.append(np.stack([
            incl,
            ri == ci,
            -1.0 * (strict & (x < 16)),
            strict & ((x >> 4) == 1),
            strict & ((x >> 5) == 1),
            strict & ((x >> 6) == 1),
        ]).astype(np.float32))
    return jnp.asarray(np.concatenate(halves, axis=2))


def _gdn_body(q_ref, k_ref, v_ref, gate_ref, g_ref, gt_ref, cq_ref, ck_ref, cv_ref, ng_ref, m_ref,
              y_ref,
              qn_s, kn_s, vn_s, qk0_s, kk_s,
              p_s, bd_s, off_s, qkd_s, rhs_s, kg_s, qg_s, uw_s,
              xq_s, c_s, op_s, eg_s, sbd_s, o_s, *, seq):
    c = GDN_CHUNK
    n_chunks = seq // c
    group = min(GDN_GROUP, n_chunks)
    h = pl.program_id(1)
    lane = lax.broadcasted_iota(jnp.int32, (c, LANE), 1)
    sub8 = lax.broadcasted_iota(jnp.int32, (SUBLANE, c), 0)
    lo, hi = slice(0, c), slice(c, 2 * c)
    halves = (lo, hi)

    bd_s[...] = jnp.zeros(bd_s.shape, BF16)
    sbd_s[...] = jnp.zeros(sbd_s.shape, BF16)

    def conv_silu(x_ref, cw_ref, ci_, c0):
        halo = SUBLANE
        prev0 = pl.multiple_of(jnp.maximum(c0 - halo, 0), SUBLANE)
        next0 = pl.multiple_of(jnp.minimum(c0 + c, seq - halo), SUBLANE)
        prev = jnp.where(ci_ > 0, x_ref[0, pl.ds(prev0, halo), :], 0.0)
        nxt = jnp.where(ci_ < n_chunks - 1, x_ref[0, pl.ds(next0, halo), :], 0.0)
        win = jnp.concatenate([prev, x_ref[0, pl.ds(c0, c), :], nxt], axis=0)
        cw = cw_ref[...]
        acc = None
        for j in range(CONV_K):
            start = halo + j - (CONV_K - 1) // 2
            term = win[start:start + c, :] * cw[j:j + 1, :]
            acc = term if acc is None else acc + term
        return acc * jax.nn.sigmoid(acc)

    def l2n(x):
        return x * lax.rsqrt(jnp.sum(x * x, axis=-1, keepdims=True) + EPS)

    def column(gc, idx):
        return jnp.sum(jnp.where(lane == idx, gc, 0.0), axis=1, keepdims=True)

    def gt_row(ci_, idx):
        base = pl.multiple_of((idx >> 3) << 3, SUBLANE)
        rows = gt_ref[0, ci_, pl.ds(base, SUBLANE), :]
        return jnp.sum(jnp.where(sub8 == (idx & 7), rows, 0.0), axis=0, keepdims=True)

    def phase0(ci_, carry):
        c0 = pl.multiple_of(ci_ * c, c)
        q = l2n(conv_silu(q_ref, cq_ref, ci_, c0)) * (HEAD_DIM ** -0.5)
        k = l2n(conv_silu(k_ref, ck_ref, ci_, c0))
        v = conv_silu(v_ref, cv_ref, ci_, c0)
        kb = k.astype(BF16)
        qkk = lax.dot_general(jnp.concatenate([q.astype(BF16), kb], axis=0), kb,
                              (((1,), (1,)), ((), ())), preferred_element_type=F32)
        qn_s[ci_], kn_s[ci_], vn_s[ci_] = q, k, v
        qk0_s[ci_], kk_s[ci_] = qkk[:c], qkk[c:]
        return carry

    lax.fori_loop(0, n_chunks, phase0, 0, unroll=2)

    def set_bd(buf, j, x):
        buf[j, lo, lo] = x[:, lo]
        buf[j, hi, hi] = x[:, hi]

    def packed_lhs(buf, j):
        return jnp.concatenate([buf[j, lo, lo], buf[j, hi, hi]], axis=1)

    def build(j, slot):
        kd, qkd = [], []
        for d in range(2):
            ci_ = slot if d == 0 else n_chunks - 1 - slot
            gc = g_ref[0, pl.ds(pl.multiple_of(ci_ * c, c), c), :]
            beta = column(gc, G_BETA + GDN_HEADS * d + h)
            cum_c = column(gc, G_CUM + GDN_HEADS * d + h)
            cum_r = gt_row(ci_, G_CUM + GDN_HEADS * d + h)
            tot_r = gt_row(ci_, G_TOT + GDN_HEADS * d + h)
            e = jnp.exp(jnp.minimum(cum_c - cum_r, 0.0))
            e_c = jnp.exp(cum_c)
            q, k, v = qn_s[ci_], kn_s[ci_], vn_s[ci_]
            kd.append(kk_s[ci_] * beta * e)
            qkd.append(qk0_s[ci_] * e)
            rhs_s[j, d] = jnp.concatenate([v * beta, k * (beta * e_c)], axis=1).astype(BF16)
            kg_s[j, d] = (k * jnp.exp(tot_r - cum_c)).astype(BF16)
            qg_s[j, :, halves[d]] = q * e_c
            eg_s[slot, :, halves[d]] = jnp.broadcast_to(jnp.exp(tot_r), (SUBLANE, c))
        kd = jnp.concatenate(kd, axis=1)
        qkd_s[j] = (jnp.concatenate(qkd, axis=1) * m_ref[M_INCL]).astype(BF16)
        n1 = kd * m_ref[M_NEG_DIAG16]
        p_s[j] = m_ref[M_EYE] + n1
        set_bd(bd_s, j, n1.astype(BF16))
        for lvl, m in enumerate((M_OFF16, M_OFF32, M_OFF64)):
            off_s[j, lvl] = (kd * m_ref[m]).astype(BF16)

    def neumann_first(j):
        n2 = jnp.dot(packed_lhs(bd_s, j), bd_s[j], preferred_element_type=F32)
        set_bd(bd_s, j, n2.astype(BF16))

    def neumann_step(j, last):
        n = packed_lhs(bd_s, j)
        rhs = bd_s[j]
        p = p_s[j]
        p_s[j] = p + jnp.dot(p.astype(BF16), rhs, preferred_element_type=F32)
        if not last:
            set_bd(bd_s, j, jnp.dot(n, rhs, preferred_element_type=F32).astype(BF16))

    def merge_a(j, lvl):
        set_bd(bd_s, j, p_s[j].astype(BF16))
        x = jnp.dot(off_s[j, lvl], bd_s[j], preferred_element_type=F32)
        set_bd(bd_s, j, x.astype(BF16))

    def merge_b(j):
        t = p_s[j]
        p_s[j] = t - jnp.dot(t.astype(BF16), bd_s[j], preferred_element_type=F32)

    def apply_t(j):
        t = p_s[j].astype(BF16)
        for d in range(2):
            uw = jnp.dot(t[:, halves[d]], rhs_s[j, d], preferred_element_type=F32)
            uw_s[j, d] = uw.astype(BF16)

    def finish(j, slot):
        for d in range(2):
            uw = uw_s[j, d]
            cx = lax.dot_general(kg_s[j, d], uw, (((0,), (0,)), ((), ())), preferred_element_type=F32)
            ow = jnp.dot(qkd_s[j, :, halves[d]], uw, preferred_element_type=F32)
            c_s[slot, :, halves[d]] = cx[:, lo]
            op_s[slot, :, halves[d]] = ow[:, lo]
            xq_s[slot, lo, halves[d]] = cx[:, hi].astype(BF16)
            xq_s[slot, hi, halves[d]] = (qg_s[j, :, halves[d]] - ow[:, hi]).astype(BF16)

    def prep_group(gi, carry):
        base = gi * group
        chains = range(group)
        for j in chains:
            build(j, base + j)
        for j in chains:
            neumann_first(j)
        for step in range(3):
            for j in chains:
                neumann_step(j, last=step == 2)
        for lvl in range(3):
            for j in chains:
                merge_a(j, lvl)
            for j in chains:
                merge_b(j)
        for j in chains:
            apply_t(j)
        for j in chains:
            finish(j, base + j)
        return carry

    lax.fori_loop(0, n_chunks // group, prep_group, 0)

    state = jnp.zeros((c, 2 * c), F32)
    for s in range(n_chunks):
        buf = s % 2
        set_bd(sbd_s, buf, state.astype(BF16))
        rhs = sbd_s[buf]
        xs = jnp.dot(xq_s[s, lo, :], rhs, preferred_element_type=F32)
        out = jnp.dot(xq_s[s, hi, :], rhs, preferred_element_type=F32) + op_s[s]
        o_s[0, s * c:(s + 1) * c, :] = out[:, lo]
        o_s[1, (n_chunks - 1 - s) * c:(n_chunks - s) * c, :] = out[:, hi]
        state = state * eg_s[s, 0:1, :] - xs + c_s[s]

    o = o_s[0] + o_s[1]
    o = o * lax.rsqrt(jnp.mean(o * o, axis=-1, keepdims=True) + EPS) * ng_ref[...]
    gate = gate_ref[0]
    y_ref[0] = (o * (gate * jax.nn.sigmoid(gate))).astype(y_ref.dtype)


def _gdn(z3, g3, gt4, conv_w, norm_g):
    b, s, _ = z3.shape
    c = GDN_CHUNK
    n_chunks = s // c
    group = min(GDN_GROUP, n_chunks)
    blk0 = 3 * W_A // HEAD_DIM
    head = lambda k: pl.BlockSpec((1, s, HEAD_DIM), lambda i, h, k=k: (i, 0, blk0 + k * GDN_HEADS + h))
    cw = lambda k: pl.BlockSpec((CONV_K, HEAD_DIM), lambda i, h, k=k: (0, k * GDN_HEADS + h))
    return pl.pallas_call(
        functools.partial(_gdn_body, seq=s),
        grid=(b, GDN_HEADS),
        in_specs=[
            head(0), head(1), head(2), head(3),
            pl.BlockSpec((1, s, LANE), lambda i, h: (i, 0, 0)),
            pl.BlockSpec((1, n_chunks, LANE, c), lambda i, h: (i, 0, 0, 0)),
            cw(0), cw(1), cw(2),
            pl.BlockSpec((1, HEAD_DIM), lambda i, h: (0, 0)),
            pl.BlockSpec((6, c, 2 * c), lambda i, h: (0, 0, 0)),
        ],
        out_specs=pl.BlockSpec((1, s, HEAD_DIM), lambda i, h: (i, 0, h)),
        out_shape=jax.ShapeDtypeStruct((b, s, W_B), BF16),
        scratch_shapes=[
            pltpu.VMEM((n_chunks, c, HEAD_DIM), F32),
            pltpu.VMEM((n_chunks, c, HEAD_DIM), F32),
            pltpu.VMEM((n_chunks, c, HEAD_DIM), F32),
            pltpu.VMEM((n_chunks, c, c), F32),
            pltpu.VMEM((n_chunks, c, c), F32),
            pltpu.VMEM((group, c, 2 * c), F32),
            pltpu.VMEM((group, 2 * c, 2 * c), BF16),
            pltpu.VMEM((group, 3, c, 2 * c), BF16),
            pltpu.VMEM((group, c, 2 * c), BF16),
            pltpu.VMEM((group, 2, c, 2 * c), BF16),
            pltpu.VMEM((group, 2, c, HEAD_DIM), BF16),
            pltpu.VMEM((group, c, 2 * c), F32),
            pltpu.VMEM((group, 2, c, 2 * c), BF16),
            pltpu.VMEM((n_chunks, 2 * c, 2 * c), BF16),
            pltpu.VMEM((n_chunks, c, 2 * c), F32),
            pltpu.VMEM((n_chunks, c, 2 * c), F32),
            pltpu.VMEM((n_chunks, SUBLANE, 2 * c), F32),
            pltpu.VMEM((2, 2 * c, 2 * c), BF16),
            pltpu.VMEM((2, s, HEAD_DIM), F32),
        ],
        compiler_params=_params("parallel", "arbitrary"),
        name="gdn",
    )(z3, z3, z3, z3, g3, gt4, conv_w, conv_w, conv_w, norm_g.reshape(1, HEAD_DIM), _gdn_masks())


def _out_proj_body(ya_ref, yb_ref, yc_ref, wa_ref, wb_ref, wc_ref, x_ref, o_ref):
    acc = jnp.dot(ya_ref[...], wa_ref[...], preferred_element_type=F32)
    acc += jnp.dot(yb_ref[...], wb_ref[...], preferred_element_type=F32)
    acc += jnp.dot(yc_ref[...], wc_ref[...], preferred_element_type=F32)
    o_ref[...] = x_ref[...] + acc


def _out_proj(ya, yb, yc, w_out, x, *, tm, tn):
    t, d = x.shape
    tm = min(tm, t)
    wa, wb, wc = w_out[:W_A], w_out[W_A:W_A + W_B], w_out[W_A + W_B:]
    row = lambda w: pl.BlockSpec((tm, w), lambda i, j: (i, 0))
    wcol = lambda w: pl.BlockSpec((w, tn), lambda i, j: (0, j))
    return pl.pallas_call(
        _out_proj_body,
        grid=(t // tm, d // tn),
        in_specs=[row(W_A), row(W_B), row(W_C), wcol(W_A), wcol(W_B), wcol(W_C),
                  pl.BlockSpec((tm, tn), lambda i, j: (i, j))],
        out_specs=pl.BlockSpec((tm, tn), lambda i, j: (i, j)),
        out_shape=jax.ShapeDtypeStruct((t, d), F32),
        compiler_params=_params("parallel", "parallel"),
        name="out_proj",
    )(ya, yb, yc, wa, wb, wc, x)


def _rms_body(x_ref, g_ref, o_ref):
    x = x_ref[...]
    o_ref[...] = x * lax.rsqrt(jnp.mean(x * x, axis=-1, keepdims=True) + EPS) * g_ref[...]


def _rms_norm(x, g, *, tm):
    t, d = x.shape
    tm = min(tm, t)
    return pl.pallas_call(
        _rms_body,
        grid=(t // tm,),
        in_specs=[pl.BlockSpec((tm, d), lambda i: (i, 0)), pl.BlockSpec((1, d), lambda i: (0, 0))],
        out_specs=pl.BlockSpec((tm, d), lambda i: (i, 0)),
        out_shape=jax.ShapeDtypeStruct((t, d), F32),
        compiler_params=_params("parallel"),
        name="final_norm",
    )(x, g.reshape(1, d))


def _lane_row(values, offset):
    flat = values.reshape(-1).astype(F32)
    return jnp.zeros((1, LANE), F32).at[0, offset:offset + flat.shape[0]].set(flat)


def _layer(x, mem, norm_g, w_in, sgu_ln_g, sgu_ln_b, sgu_w, sgu_b, conv_w, a_log, dt_bias,
           gdn_norm_g, mem_norm_g, w_mem_kv, w_out):
    b, s, d = x.shape
    t = b * s
    x2 = x.reshape(t, d)
    w_main = jnp.concatenate([w_in[:, :OFF_SMALL], w_in[:, OFF_SMALL + N_SMALL:]], axis=1).astype(BF16)
    w_small = jnp.pad(w_in[:, OFF_SMALL:OFF_SMALL + N_SMALL], ((0, 0), (0, LANE - N_SMALL))).astype(BF16)
    z, zs = _norm_matmul(x2, norm_g, w_main, w_small, tm=512, tn=1280, out_dtype=F32)
    z3 = z.reshape(b, s, W_MAIN)

    g, gt = _gate_prep(zs, _lane_row(a_log, G_CUM), _lane_row(dt_bias, G_CUM), rows=512)
    g3 = g.reshape(b, s, LANE)
    gt4 = gt.reshape(b, s // GDN_CHUNK, LANE, GDN_CHUNK)

    ya = _sgu(z3, sgu_ln_g, sgu_ln_b, sgu_w, sgu_b, rows=512)
    yb = _gdn(z3, g3, gt4, conv_w, gdn_norm_g)
    kv = _norm_matmul(mem.reshape(-1, d), mem_norm_g, w_mem_kv.astype(BF16), tm=512, tn=1024,
                      out_dtype=BF16)
    yc = _xattn(z3, kv.reshape(b, -1, 2 * W_C), rows=512)

    out = _out_proj(ya.reshape(t, W_A), yb.reshape(t, W_B), yc.reshape(t, W_C),
                    w_out.astype(BF16), x2, tm=1024, tn=512)
    return out.reshape(b, s, d)


def kernel(x_prompt, x_sample, mem_prompt, mem_sample, norm_g, w_in, sgu_ln_g, sgu_ln_b, sgu_w,
           sgu_b, conv_w, a_log, dt_bias, gdn_norm_g, mem_norm_g, w_mem_kv, w_out, final_g):
    n_prompt = x_prompt.shape[0]
    x = jnp.concatenate([x_prompt, x_sample], axis=0)
    mem = jnp.concatenate([mem_prompt, mem_sample], axis=0)
    for l in range(norm_g.shape[0]):
        x = _layer(x, mem, norm_g[l], w_in[l], sgu_ln_g[l], sgu_ln_b[l], sgu_w[l], sgu_b[l],
                   conv_w[l], a_log[l], dt_bias[l], gdn_norm_g[l], mem_norm_g[l], w_mem_kv[l],
                   w_out[l])
    b, s, d = x.shape
    y = _rms_norm(x.reshape(b * s, d), final_g, tm=512).reshape(b, s, d)
    return y[:n_prompt], y[n_prompt:]
```

```python
import functools

import jax
import jax.numpy as jnp
import numpy as np
from jax import lax
from jax.experimental import pallas as pl
from jax.experimental.pallas import tpu as pltpu

F32 = jnp.float32
BF16 = jnp.bfloat16
EPS = 1e-6

SGU_CHUNK = 128
SGU_GROUPS = 12
GDN_HEADS = 12
HEAD_DIM = 128
CONV_K = 5
XA_HEADS = 4
XA_HEAD_DIM = 256
W_A = SGU_GROUPS * SGU_CHUNK
W_B = GDN_HEADS * HEAD_DIM
W_C = XA_HEADS * XA_HEAD_DIM
W_MAIN = 3 * W_A + 4 * W_B + 2 * W_C
N_SMALL = 4 * GDN_HEADS
OFF_SMALL = 3 * W_A + 4 * W_B

LANE = 128
SUBLANE = 8
VMEM_LIMIT = 56 * 1024 * 1024

GDN_CHUNK = 128
G_BETA, G_CUM, G_TOT = 0, 2 * GDN_HEADS, 4 * GDN_HEADS
M_BOUND, M_EYE, M_NEG_DIAG16, M_OFF16, M_OFF32, M_OFF64 = range(6)
MASKED_OUT = -1e30


def _params(*sem):
    return pltpu.CompilerParams(dimension_semantics=sem, vmem_limit_bytes=VMEM_LIMIT)


def _norm_matmul_body(*refs, has_small):
    if has_small:
        x_ref, g_ref, w_ref, ws_ref, z_ref, zs_ref, h_ref = refs
    else:
        x_ref, g_ref, w_ref, z_ref, h_ref = refs

    @pl.when(pl.program_id(1) == 0)
    def _():
        x = x_ref[...]
        ms = jnp.mean(x * x, axis=-1, keepdims=True)
        h = (x * lax.rsqrt(ms + EPS) * g_ref[...]).astype(BF16)
        h_ref[...] = h
        if has_small:
            zs_ref[...] = jnp.dot(h, ws_ref[...], preferred_element_type=F32)

    z_ref[...] = jnp.dot(h_ref[...], w_ref[...], preferred_element_type=F32).astype(z_ref.dtype)


def _norm_matmul(x, g, w, w_small=None, *, tm, tn, out_dtype):
    t, d = x.shape
    n = w.shape[1]
    tm = min(tm, t)
    has_small = w_small is not None
    in_specs = [
        pl.BlockSpec((tm, d), lambda i, j: (i, 0)),
        pl.BlockSpec((1, d), lambda i, j: (0, 0)),
        pl.BlockSpec((d, tn), lambda i, j: (0, j)),
    ]
    out_shape = [jax.ShapeDtypeStruct((t, n), out_dtype)]
    out_specs = [pl.BlockSpec((tm, tn), lambda i, j: (i, j))]
    args = [x, g.reshape(1, d), w]
    if has_small:
        in_specs.append(pl.BlockSpec((d, LANE), lambda i, j: (0, 0)))
        out_shape.append(jax.ShapeDtypeStruct((t, LANE), F32))
        out_specs.append(pl.BlockSpec((tm, LANE), lambda i, j: (i, 0)))
        args.append(w_small)
    outs = pl.pallas_call(
        functools.partial(_norm_matmul_body, has_small=has_small),
        grid=(t // tm, n // tn),
        in_specs=in_specs,
        out_specs=out_specs,
        out_shape=out_shape,
        scratch_shapes=[pltpu.VMEM((tm, d), BF16)],
        compiler_params=_params("parallel", "arbitrary"),
        name="norm_matmul_small" if has_small else "norm_matmul",
    )(*args)
    return outs if has_small else outs[0]


def _gate_prep_body(zs_ref, a_ref, dt_ref, g_ref, gt_ref, *, chunks):
    c = GDN_CHUNK
    lane = lax.broadcasted_iota(jnp.int32, (c, LANE), 1)
    ri = lax.broadcasted_iota(jnp.int32, (c, c), 0)
    ci = lax.broadcasted_iota(jnp.int32, (c, c), 1)
    lower = (ri >= ci).astype(F32)
    upper = (ri <= ci).astype(F32)
    is_decay = (lane >= G_CUM) & (lane < G_TOT)
    neg_a = -jnp.exp(a_ref[...])
    for n in range(chunks):
        zs = zs_ref[pl.ds(n * c, c), :]
        beta = jax.nn.sigmoid(zs)
        g = jnp.where(is_decay, neg_a * jax.nn.softplus(zs + dt_ref[...]), 0.0)
        cum_fw = jnp.dot(lower, g, preferred_element_type=F32, precision=lax.Precision.HIGHEST)
        cum_bw = jnp.dot(upper, g, preferred_element_type=F32, precision=lax.Precision.HIGHEST)
        tot = jnp.broadcast_to(jnp.sum(g, axis=0, keepdims=True), (c, LANE))
        tot = pltpu.roll(tot, G_TOT - G_CUM, 1)
        out = jnp.where(lane < G_CUM, beta,
                        jnp.where(lane < G_CUM + GDN_HEADS, cum_fw,
                                  jnp.where(lane < G_TOT, cum_bw,
                                            jnp.where(lane < G_TOT + 2 * GDN_HEADS, tot, 0.0))))
        g_ref[pl.ds(n * c, c), :] = out
        gt_ref[n] = out.T


def _gate_prep(zs, a_row, dt_row, *, rows):
    t = zs.shape[0]
    rows = min(rows, t)
    chunks = rows // GDN_CHUNK
    return pl.pallas_call(
        functools.partial(_gate_prep_body, chunks=chunks),
        grid=(t // rows,),
        in_specs=[
            pl.BlockSpec((rows, LANE), lambda i: (i, 0)),
            pl.BlockSpec((1, LANE), lambda i: (0, 0)),
            pl.BlockSpec((1, LANE), lambda i: (0, 0)),
        ],
        out_specs=[
            pl.BlockSpec((rows, LANE), lambda i: (i, 0)),
            pl.BlockSpec((chunks, LANE, GDN_CHUNK), lambda i: (i, 0, 0)),
        ],
        out_shape=[
            jax.ShapeDtypeStruct((t, LANE), F32),
            jax.ShapeDtypeStruct((t // GDN_CHUNK, LANE, GDN_CHUNK), F32),
        ],
        compiler_params=_params("parallel"),
        name="gate_prep",
    )(zs, a_row, dt_row)


def _sgu_body(u_ref, v_ref, gate_ref, lg_ref, lb_ref, ws_ref, bs_ref, y_ref, *, chunks):
    c = SGU_CHUNK
    v = jax.nn.gelu(v_ref[0])
    mu = jnp.mean(v, axis=-1, keepdims=True)
    vc = v - mu
    var = jnp.mean(vc * vc, axis=-1, keepdims=True)
    vn = (vc * lax.rsqrt(var + EPS) * lg_ref[...] + lb_ref[...]).astype(BF16)
    for n in range(chunks):
        for g in range(SGU_GROUPS):
            rows, cols = slice(n * c, (n + 1) * c), slice(g * c, (g + 1) * c)
            mixed = jnp.dot(ws_ref[g], vn[rows, cols], preferred_element_type=F32) + bs_ref[g]
            gate = gate_ref[0, rows, cols]
            y = jax.nn.gelu(u_ref[0, rows, cols]) * mixed * (gate * jax.nn.sigmoid(gate))
            y_ref[0, rows, cols] = y.astype(y_ref.dtype)


def _sgu(z3, ln_g, ln_b, w_s, b_s, *, rows):
    b, s, _ = z3.shape
    rows = min(rows, s)
    chunks = rows // SGU_CHUNK
    col = lambda k: pl.BlockSpec((1, rows, W_A), lambda i, r, k=k: (i, r, k))
    return pl.pallas_call(
        functools.partial(_sgu_body, chunks=chunks),
        grid=(b, s // rows),
        in_specs=[
            col(0), col(1), col(2),
            pl.BlockSpec((1, W_A), lambda i, r: (0, 0)),
            pl.BlockSpec((1, W_A), lambda i, r: (0, 0)),
            pl.BlockSpec((SGU_GROUPS, SGU_CHUNK, SGU_CHUNK), lambda i, r: (0, 0, 0)),
            pl.BlockSpec((SGU_GROUPS, SGU_CHUNK, LANE), lambda i, r: (0, 0, 0)),
        ],
        out_specs=pl.BlockSpec((1, rows, W_A), lambda i, r: (i, r, 0)),
        out_shape=jax.ShapeDtypeStruct((b, s, W_A), BF16),
        compiler_params=_params("parallel", "parallel"),
        name="sgu",
    )(z3, z3, z3, ln_g.reshape(1, W_A), ln_b.reshape(1, W_A), w_s.astype(BF16),
      jnp.broadcast_to(b_s[:, :, None], (SGU_GROUPS, SGU_CHUNK, LANE)))


def _xattn_body(q_ref, gate_ref, k_ref, v_ref, y_ref):
    q = q_ref[0].astype(BF16)
    s = lax.dot_general(q, k_ref[0], (((1,), (1,)), ((), ())), preferred_element_type=F32)
    s = s * (XA_HEAD_DIM ** -0.5)
    s = s - jnp.max(s, axis=-1, keepdims=True)
    p = jnp.exp(s)
    p = p / jnp.sum(p, axis=-1, keepdims=True)
    o = jnp.dot(p.astype(BF16), v_ref[0], preferred_element_type=F32)
    gate = gate_ref[0]
    y_ref[0] = (o * (gate * jax.nn.sigmoid(gate))).astype(y_ref.dtype)


def _xattn(z3, kv3, *, rows):
    b, s, _ = z3.shape
    rows = min(rows, s)
    n_mem = kv3.shape[1]
    q_blk = (3 * W_A + 4 * W_B) // XA_HEAD_DIM
    gate_blk = q_blk + XA_HEADS
    return pl.pallas_call(
        _xattn_body,
        grid=(b, s // rows, XA_HEADS),
        in_specs=[
            pl.BlockSpec((1, rows, XA_HEAD_DIM), lambda i, r, h: (i, r, q_blk + h)),
            pl.BlockSpec((1, rows, XA_HEAD_DIM), lambda i, r, h: (i, r, gate_blk + h)),
            pl.BlockSpec((1, n_mem, XA_HEAD_DIM), lambda i, r, h: (i, 0, h)),
            pl.BlockSpec((1, n_mem, XA_HEAD_DIM), lambda i, r, h: (i, 0, XA_HEADS + h)),
        ],
        out_specs=pl.BlockSpec((1, rows, XA_HEAD_DIM), lambda i, r, h: (i, r, h)),
        out_shape=jax.ShapeDtypeStruct((b, s, W_C), BF16),
        compiler_params=_params("parallel", "parallel", "parallel"),
        name="xattn",
    )(z3, z3, kv3, kv3)


def _gdn_masks():
    c = GDN_CHUNK
    ri = np.arange(c)[:, None]
    ci = np.arange(c)[None, :]
    x = ri ^ ci
    halves = []
    for strict, incl in ((ri > ci, ri >= ci), (ri < ci, ri <= ci)):
        halves.append(np.stack([
            np.where(incl, 0.0, MASKED_OUT),
            ri == ci,
            -1.0 * (strict & (x < 16)),
            strict & ((x >> 4) == 1),
            strict & ((x >> 5) == 1),
            strict & ((x >> 6) == 1),
        ]).astype(np.float32))
    masks = np.concatenate(halves, axis=2)
    return jnp.asarray(masks), jnp.asarray(masks[M_NEG_DIAG16:], dtype=BF16)


def _gdn_body(q_ref, k_ref, v_ref, gate_ref, g_ref, gt_ref, cq_ref, ck_ref, cv_ref, ng_ref, m_ref,
              mb_ref, y_ref,
              qn_s, kn_s, vn_s, qk0_s, kk_s,
              p_s, bd_s, off_s, qkd_s, rhs_s, kg_s, qg_s, uw_s,
              xq_s, c_s, op_s, eg_s, sbd_s, st_s, o_s, *, seq):
    c = GDN_CHUNK
    n_chunks = seq // c
    h = pl.program_id(1)
    lane = lax.broadcasted_iota(jnp.int32, (c, LANE), 1)
    lo, hi = slice(0, c), slice(c, 2 * c)
    halves = (lo, hi)

    bd_s[...] = jnp.zeros(bd_s.shape, BF16)
    sbd_s[...] = jnp.zeros(sbd_s.shape, BF16)
    st_s[...] = jnp.zeros(st_s.shape, F32)

    def conv_silu(x_ref, cw_ref, ci_):
        halo = SUBLANE
        c0 = ci_ * c
        left = (CONV_K - 1) // 2
        cw = cw_ref[...]
        if 0 < ci_ < n_chunks - 1:
            taps = [x_ref[0, c0 + j - left:c0 + j - left + c, :] for j in range(CONV_K)]
        else:
            zeros = jnp.zeros((halo, HEAD_DIM), F32)
            prev = x_ref[0, c0 - halo:c0, :] if ci_ > 0 else zeros
            nxt = x_ref[0, c0 + c:c0 + c + halo, :] if ci_ < n_chunks - 1 else zeros
            win = jnp.concatenate([prev, x_ref[0, c0:c0 + c, :], nxt], axis=0)
            taps = [win[halo + j - left:halo + j - left + c, :] for j in range(CONV_K)]
        acc = None
        for j in range(CONV_K):
            term = taps[j] * cw[j:j + 1, :]
            acc = term if acc is None else acc + term
        return acc * jax.nn.sigmoid(acc)

    def l2n(x, scale=1.0):
        return x * (lax.rsqrt(jnp.sum(x * x, axis=-1, keepdims=True) + EPS) * scale)

    def column(gc, idx):
        return jnp.sum(jnp.where(lane == idx, gc, 0.0), axis=1, keepdims=True)

    def gt_row(ci_, idx):
        return gt_ref[0, ci_, pl.ds(idx, 1), :]

    def set_bd(buf, j, x):
        buf[j, lo, lo] = x[:, lo]
        buf[j, hi, hi] = x[:, hi]

    def packed_lhs(buf, j):
        return jnp.concatenate([buf[j, lo, lo], buf[j, hi, hi]], axis=1)

    cached = set()

    def chunk_inputs(ci_):
        if ci_ in cached:
            return qn_s[ci_], kn_s[ci_], vn_s[ci_], qk0_s[ci_], kk_s[ci_]
        cached.add(ci_)
        q = l2n(conv_silu(q_ref, cq_ref, ci_), HEAD_DIM ** -0.5)
        k = l2n(conv_silu(k_ref, ck_ref, ci_))
        v = conv_silu(v_ref, cv_ref, ci_)
        kb = k.astype(BF16)
        qkk = lax.dot_general(jnp.concatenate([q.astype(BF16), kb], axis=0), kb,
                              (((1,), (1,)), ((), ())), preferred_element_type=F32)
        qn_s[ci_], kn_s[ci_], vn_s[ci_] = q, k, v
        qk0_s[ci_], kk_s[ci_] = qkk[:c], qkk[c:]
        return q, k, v, qkk[:c], qkk[c:]

    def build(s):
        kd, qkd = [], []
        for d in range(2):
            ci_ = s if d == 0 else n_chunks - 1 - s
            q, k, v, qk0, kk = chunk_inputs(ci_)
            gc = g_ref[0, ci_ * c:(ci_ + 1) * c, :]
            beta = column(gc, G_BETA + GDN_HEADS * d + h)
            cum_c = column(gc, G_CUM + GDN_HEADS * d + h)
            cum_r = gt_row(ci_, G_CUM + GDN_HEADS * d + h)
            tot_r = gt_row(ci_, G_TOT + GDN_HEADS * d + h)
            e = jnp.exp(jnp.minimum(cum_c - cum_r, m_ref[M_BOUND, :, halves[d]]))
            e_c = jnp.exp(cum_c)
            kd.append(kk * beta * e)
            qkd.append(qk0 * e)
            rhs_s[s, d] = jnp.concatenate([v * beta, k * (beta * e_c)], axis=1).astype(BF16)
            kg_s[s, d] = (k * jnp.exp(tot_r - cum_c)).astype(BF16)
            qg_s[s, :, halves[d]] = q * e_c
            eg_s[s, :, halves[d]] = jnp.broadcast_to(jnp.exp(tot_r), (SUBLANE, c))
        kd = jnp.concatenate(kd, axis=1)
        qkd_s[s] = jnp.concatenate(qkd, axis=1).astype(BF16)
        p_s[s] = m_ref[M_EYE] + kd * m_ref[M_NEG_DIAG16]
        kd = kd.astype(BF16)
        set_bd(bd_s, s, kd * mb_ref[0])
        for lvl in range(3):
            off_s[s, lvl] = kd * mb_ref[1 + lvl]

    def neumann_first(j):
        n2 = jnp.dot(packed_lhs(bd_s, j), bd_s[j], preferred_element_type=F32)
        set_bd(bd_s, j, n2.astype(BF16))

    def neumann_step(j, last):
        n = packed_lhs(bd_s, j)
        rhs = bd_s[j]
        p = p_s[j]
        p_s[j] = p + jnp.dot(p.astype(BF16), rhs, preferred_element_type=F32)
        if not last:
            set_bd(bd_s, j, jnp.dot(n, rhs, preferred_element_type=F32).astype(BF16))

    def merge_a(j, lvl):
        set_bd(bd_s, j, p_s[j].astype(BF16))
        x = jnp.dot(off_s[j, lvl], bd_s[j], preferred_element_type=F32)
        set_bd(bd_s, j, x.astype(BF16))

    def merge_b(j):
        t = p_s[j]
        p_s[j] = t - jnp.dot(t.astype(BF16), bd_s[j], preferred_element_type=F32)

    def apply_t(j):
        t = p_s[j].astype(BF16)
        for d in range(2):
            uw = jnp.dot(t[:, halves[d]], rhs_s[j, d], preferred_element_type=F32)
            uw_s[j, d] = uw.astype(BF16)

    def finish(j):
        for d in range(2):
            uw = uw_s[j, d]
            cx = lax.dot_general(kg_s[j, d], uw, (((0,), (0,)), ((), ())), preferred_element_type=F32)
            ow = jnp.dot(qkd_s[j, :, halves[d]], uw, preferred_element_type=F32)
            c_s[j, :, halves[d]] = cx[:, lo]
            op_s[j, :, halves[d]] = ow[:, lo]
            xq_s[j, lo, halves[d]] = cx[:, hi].astype(BF16)
            xq_s[j, hi, halves[d]] = (qg_s[j, :, halves[d]] - ow[:, hi]).astype(BF16)

    stages = [
        build,
        neumann_first,
        functools.partial(neumann_step, last=False),
        functools.partial(neumann_step, last=False),
        functools.partial(neumann_step, last=True),
        functools.partial(merge_a, lvl=0), merge_b,
        functools.partial(merge_a, lvl=1), merge_b,
        functools.partial(merge_a, lvl=2), merge_b,
        apply_t,
        finish,
    ]

    def scan_step(s):
        buf = s % 2
        state = st_s[...]
        set_bd(sbd_s, buf, state.astype(BF16))
        rhs = sbd_s[buf]
        xs = jnp.dot(xq_s[s, lo, :], rhs, preferred_element_type=F32)
        out = jnp.dot(xq_s[s, hi, :], rhs, preferred_element_type=F32) + op_s[s]
        o_s[0, s * c:(s + 1) * c, :] = out[:, lo]
        o_s[1, (n_chunks - 1 - s) * c:(n_chunks - s) * c, :] = out[:, hi]
        st_s[...] = state * eg_s[s, 0:1, :] - xs + c_s[s]

    for t in range(n_chunks + len(stages) - 1):
        for s in range(n_chunks):
            if 0 <= t - s < len(stages):
                stages[t - s](s)
        if t >= len(stages) - 1:
            scan_step(t - (len(stages) - 1))

    o = o_s[0] + o_s[1]
    o = o * lax.rsqrt(jnp.mean(o * o, axis=-1, keepdims=True) + EPS) * ng_ref[...]
    gate = gate_ref[0]
    y_ref[0] = (o * (gate * jax.nn.sigmoid(gate))).astype(y_ref.dtype)


def _gdn(z3, g3, gt4, conv_w, norm_g):
    b, s, _ = z3.shape
    c = GDN_CHUNK
    n_chunks = s // c
    blk0 = 3 * W_A // HEAD_DIM
    head = lambda k: pl.BlockSpec((1, s, HEAD_DIM), lambda i, h, k=k: (i, 0, blk0 + k * GDN_HEADS + h))
    cw = lambda k: pl.BlockSpec((CONV_K, HEAD_DIM), lambda i, h, k=k: (0, k * GDN_HEADS + h))
    return pl.pallas_call(
        functools.partial(_gdn_body, seq=s),
        grid=(b, GDN_HEADS),
        in_specs=[
            head(0), head(1), head(2), head(3),
            pl.BlockSpec((1, s, LANE), lambda i, h: (i, 0, 0)),
            pl.BlockSpec((1, n_chunks, LANE, c), lambda i, h: (i, 0, 0, 0)),
            cw(0), cw(1), cw(2),
            pl.BlockSpec((1, HEAD_DIM), lambda i, h: (0, 0)),
            pl.BlockSpec((6, c, 2 * c), lambda i, h: (0, 0, 0)),
            pl.BlockSpec((4, c, 2 * c), lambda i, h: (0, 0, 0)),
        ],
        out_specs=pl.BlockSpec((1, s, HEAD_DIM), lambda i, h: (i, 0, h)),
        out_shape=jax.ShapeDtypeStruct((b, s, W_B), BF16),
        scratch_shapes=[
            pltpu.VMEM((n_chunks, c, HEAD_DIM), F32),
            pltpu.VMEM((n_chunks, c, HEAD_DIM), F32),
            pltpu.VMEM((n_chunks, c, HEAD_DIM), F32),
            pltpu.VMEM((n_chunks, c, c), F32),
            pltpu.VMEM((n_chunks, c, c), F32),
            pltpu.VMEM((n_chunks, c, 2 * c), F32),
            pltpu.VMEM((n_chunks, 2 * c, 2 * c), BF16),
            pltpu.VMEM((n_chunks, 3, c, 2 * c), BF16),
            pltpu.VMEM((n_chunks, c, 2 * c), BF16),
            pltpu.VMEM((n_chunks, 2, c, 2 * c), BF16),
            pltpu.VMEM((n_chunks, 2, c, HEAD_DIM), BF16),
            pltpu.VMEM((n_chunks, c, 2 * c), F32),
            pltpu.VMEM((n_chunks, 2, c, 2 * c), BF16),
            pltpu.VMEM((n_chunks, 2 * c, 2 * c), BF16),
            pltpu.VMEM((n_chunks, c, 2 * c), F32),
            pltpu.VMEM((n_chunks, c, 2 * c), F32),
            pltpu.VMEM((n_chunks, SUBLANE, 2 * c), F32),
            pltpu.VMEM((2, 2 * c, 2 * c), BF16),
            pltpu.VMEM((c, 2 * c), F32),
            pltpu.VMEM((2, s, HEAD_DIM), F32),
        ],
        compiler_params=_params("parallel", "arbitrary"),
        name="gdn",
    )(z3, z3, z3, z3, g3, gt4, conv_w, conv_w, conv_w, norm_g.reshape(1, HEAD_DIM), *_gdn_masks())


def _out_proj_body(ya_ref, yb_ref, yc_ref, wa_ref, wb_ref, wc_ref, x_ref, *rest, final_norm):
    acc = jnp.dot(ya_ref[...], wa_ref[...], preferred_element_type=F32)
    acc += jnp.dot(yb_ref[...], wb_ref[...], preferred_element_type=F32)
    acc += jnp.dot(yc_ref[...], wc_ref[...], preferred_element_type=F32)
    if not final_norm:
        (o_ref,) = rest
        o_ref[...] = x_ref[...] + acc
        return
    g_ref, o_ref, row_s = rest
    j = pl.program_id(1)
    n_col, _, tn = row_s.shape
    row_s[j] = x_ref[...] + acc

    @pl.when(j == n_col - 1)
    def _():
        ss = None
        for jj in range(n_col):
            part = jnp.sum(row_s[jj] * row_s[jj], axis=-1, keepdims=True)
            ss = part if ss is None else ss + part
        scale = lax.rsqrt(ss / (n_col * tn) + EPS)
        for jj in range(n_col):
            cols = slice(jj * tn, (jj + 1) * tn)
            o_ref[:, cols] = row_s[jj] * scale * g_ref[:, cols]


def _out_proj(ya, yb, yc, wa, wb, wc, x, final_g=None, *, tm, tn):
    t, d = x.shape
    tm = min(tm, t)
    final_norm = final_g is not None
    row = lambda w: pl.BlockSpec((tm, w), lambda i, j: (i, 0))
    wcol = lambda w: pl.BlockSpec((w, tn), lambda i, j: (0, j))
    in_specs = [row(W_A), row(W_B), row(W_C), wcol(W_A), wcol(W_B), wcol(W_C),
                pl.BlockSpec((tm, tn), lambda i, j: (i, j))]
    args = [ya, yb, yc, wa, wb, wc, x]
    if final_norm:
        in_specs.append(pl.BlockSpec((1, d), lambda i, j: (0, 0)))
        args.append(final_g.reshape(1, d))
        out_spec = pl.BlockSpec((tm, d), lambda i, j: (i, 0))
        scratch = [pltpu.VMEM((d // tn, tm, tn), F32)]
    else:
        out_spec = pl.BlockSpec((tm, tn), lambda i, j: (i, j))
        scratch = []
    return pl.pallas_call(
        functools.partial(_out_proj_body, final_norm=final_norm),
        grid=(t // tm, d // tn),
        in_specs=in_specs,
        out_specs=out_spec,
        out_shape=jax.ShapeDtypeStruct((t, d), F32),
        scratch_shapes=scratch,
        compiler_params=_params("parallel", "arbitrary" if final_norm else "parallel"),
        name="out_proj_norm" if final_norm else "out_proj",
    )(*args)


def _lane_row(values, offset):
    flat = values.reshape(-1).astype(F32)
    return jnp.zeros((1, LANE), F32).at[0, offset:offset + flat.shape[0]].set(flat)


def _layer_weights(w_in, w_mem_kv, w_out):
    w_main = jnp.concatenate([w_in[:, :OFF_SMALL], w_in[:, OFF_SMALL + N_SMALL:]], axis=1).astype(BF16)
    w_small = jnp.pad(w_in[:, OFF_SMALL:OFF_SMALL + N_SMALL], ((0, 0), (0, LANE - N_SMALL))).astype(BF16)
    w_out = w_out.astype(BF16)
    return w_main, w_small, w_mem_kv.astype(BF16), (w_out[:W_A], w_out[W_A:W_A + W_B], w_out[W_A + W_B:])


def _layer(x, mem, weights, norm_g, sgu_ln_g, sgu_ln_b, sgu_w, sgu_b, conv_w, a_log, dt_bias,
           gdn_norm_g, mem_norm_g, final_g):
    b, s, d = x.shape
    t = b * s
    x2 = x.reshape(t, d)
    w_main, w_small, w_kv, w_out = weights
    z, zs = _norm_matmul(x2, norm_g, w_main, w_small, tm=512, tn=1280, out_dtype=F32)
    z3 = z.reshape(b, s, W_MAIN)

    g, gt = _gate_prep(zs, _lane_row(a_log, G_CUM), _lane_row(dt_bias, G_CUM), rows=512)
    g3 = g.reshape(b, s, LANE)
    gt4 = gt.reshape(b, s // GDN_CHUNK, LANE, GDN_CHUNK)

    ya = _sgu(z3, sgu_ln_g, sgu_ln_b, sgu_w, sgu_b, rows=512)
    yb = _gdn(z3, g3, gt4, conv_w, gdn_norm_g)
    kv = _norm_matmul(mem.reshape(-1, d), mem_norm_g, w_kv, tm=512, tn=1024, out_dtype=BF16)
    yc = _xattn(z3, kv.reshape(b, -1, 2 * W_C), rows=512)

    if final_g is None:
        out = _out_proj(ya.reshape(t, W_A), yb.reshape(t, W_B), yc.reshape(t, W_C), *w_out, x2,
                        tm=1024, tn=512)
    else:
        out = _out_proj(ya.reshape(t, W_A), yb.reshape(t, W_B), yc.reshape(t, W_C), *w_out, x2,
                        final_g, tm=512, tn=512)
    return out.reshape(b, s, d)


def kernel(x_prompt, x_sample, mem_prompt, mem_sample, norm_g, w_in, sgu_ln_g, sgu_ln_b, sgu_w,
           sgu_b, conv_w, a_log, dt_bias, gdn_norm_g, mem_norm_g, w_mem_kv, w_out, final_g):
    depth = norm_g.shape[0]
    weights = [_layer_weights(w_in[l], w_mem_kv[l], w_out[l]) for l in range(depth)]

    def trunk(x, mem):
        for l in range(depth):
            x = _layer(x, mem, weights[l], norm_g[l], sgu_ln_g[l], sgu_ln_b[l], sgu_w[l], sgu_b[l],
                       conv_w[l], a_log[l], dt_bias[l], gdn_norm_g[l], mem_norm_g[l],
                       final_g if l == depth - 1 else None)
        return x

    return trunk(x_prompt, mem_prompt), trunk(x_sample, mem_sample)
```

```python
import functools

import jax
import jax.numpy as jnp
import numpy as np
from jax import lax
from jax.experimental import pallas as pl
from jax.experimental.pallas import tpu as pltpu

F32 = jnp.float32
BF16 = jnp.bfloat16
EPS = 1e-6

SGU_CHUNK = 128
SGU_GROUPS = 12
GDN_HEADS = 12
HEAD_DIM = 128
CONV_K = 5
XA_HEADS = 4
XA_HEAD_DIM = 256
W_A = SGU_GROUPS * SGU_CHUNK
W_B = GDN_HEADS * HEAD_DIM
W_C = XA_HEADS * XA_HEAD_DIM
W_MAIN = 3 * W_A + 4 * W_B + 2 * W_C
N_SMALL = 4 * GDN_HEADS
OFF_SMALL = 3 * W_A + 4 * W_B

LANE = 128
SUBLANE = 8
VMEM_LIMIT = 56 * 1024 * 1024

GDN_CHUNK = 128
BUILD_LAG = 2
G_BETA, G_CUM, G_TOT = 0, 2 * GDN_HEADS, 4 * GDN_HEADS
M_BOUND, M_EYE, M_NEG_DIAG16, M_OFF16, M_OFF32, M_OFF64 = range(6)
MASKED_OUT = -1e30


def _params(*sem):
    return pltpu.CompilerParams(dimension_semantics=sem, vmem_limit_bytes=VMEM_LIMIT)


def _norm_matmul_body(*refs, has_small):
    if has_small:
        x_ref, g_ref, w_ref, ws_ref, z_ref, zs_ref, h_ref = refs
    else:
        x_ref, g_ref, w_ref, z_ref, h_ref = refs

    @pl.when(pl.program_id(1) == 0)
    def _():
        x = x_ref[...]
        ms = jnp.mean(x * x, axis=-1, keepdims=True)
        h = (x * lax.rsqrt(ms + EPS) * g_ref[...]).astype(BF16)
        h_ref[...] = h
        if has_small:
            zs_ref[...] = jnp.dot(h, ws_ref[...], preferred_element_type=F32)

    z_ref[...] = jnp.dot(h_ref[...], w_ref[...], preferred_element_type=F32).astype(z_ref.dtype)


def _norm_matmul(x, g, w, small_block=None, *, tm, tn, out_dtype):
    t, d = x.shape
    has_small = small_block is not None
    n = small_block * LANE if has_small else w.shape[1]
    tm = min(tm, t)
    in_specs = [
        pl.BlockSpec((tm, d), lambda i, j: (i, 0)),
        pl.BlockSpec((1, d), lambda i, j: (0, 0)),
        pl.BlockSpec((d, tn), lambda i, j: (0, j)),
    ]
    out_shape = [jax.ShapeDtypeStruct((t, n), out_dtype)]
    out_specs = [pl.BlockSpec((tm, tn), lambda i, j: (i, j))]
    args = [x, g.reshape(1, d), w]
    if has_small:
        in_specs.append(pl.BlockSpec((d, LANE), lambda i, j: (0, small_block)))
        out_shape.append(jax.ShapeDtypeStruct((t, LANE), F32))
        out_specs.append(pl.BlockSpec((tm, LANE), lambda i, j: (i, 0)))
        args.append(w)
    outs = pl.pallas_call(
        functools.partial(_norm_matmul_body, has_small=has_small),
        grid=(t // tm, n // tn),
        in_specs=in_specs,
        out_specs=out_specs,
        out_shape=out_shape,
        scratch_shapes=[pltpu.VMEM((tm, d), BF16)],
        compiler_params=_params("parallel", "arbitrary"),
        name="norm_matmul_small" if has_small else "norm_matmul",
    )(*args)
    return outs if has_small else outs[0]


def _gate_prep_body(zs_ref, a_ref, dt_ref, g_ref, gt_ref, *, chunks):
    c = GDN_CHUNK
    lane = lax.broadcasted_iota(jnp.int32, (c, LANE), 1)
    ri = lax.broadcasted_iota(jnp.int32, (c, c), 0)
    ci = lax.broadcasted_iota(jnp.int32, (c, c), 1)
    lower = (ri >= ci).astype(F32)
    upper = (ri <= ci).astype(F32)
    is_decay = (lane >= G_CUM) & (lane < G_TOT)
    neg_a = -jnp.exp(a_ref[...])
    for n in range(chunks):
        zs = zs_ref[pl.ds(n * c, c), :]
        beta = jax.nn.sigmoid(zs)
        g = jnp.where(is_decay, neg_a * jax.nn.softplus(zs + dt_ref[...]), 0.0)
        cum_fw = jnp.dot(lower, g, preferred_element_type=F32, precision=lax.Precision.HIGHEST)
        cum_bw = jnp.dot(upper, g, preferred_element_type=F32, precision=lax.Precision.HIGHEST)
        tot = jnp.broadcast_to(jnp.sum(g, axis=0, keepdims=True), (c, LANE))
        tot = pltpu.roll(tot, G_TOT - G_CUM, 1)
        out = jnp.where(lane < G_CUM, beta,
                        jnp.where(lane < G_CUM + GDN_HEADS, cum_fw,
                                  jnp.where(lane < G_TOT, cum_bw,
                                            jnp.where(lane < G_TOT + 2 * GDN_HEADS, tot, 0.0))))
        g_ref[pl.ds(n * c, c), :] = out
        gt_ref[n] = out.T


def _gate_prep(zs, a_row, dt_row, *, rows):
    t = zs.shape[0]
    rows = min(rows, t)
    chunks = rows // GDN_CHUNK
    return pl.pallas_call(
        functools.partial(_gate_prep_body, chunks=chunks),
        grid=(t // rows,),
        in_specs=[
            pl.BlockSpec((rows, LANE), lambda i: (i, 0)),
            pl.BlockSpec((1, LANE), lambda i: (0, 0)),
            pl.BlockSpec((1, LANE), lambda i: (0, 0)),
        ],
        out_specs=[
            pl.BlockSpec((rows, LANE), lambda i: (i, 0)),
            pl.BlockSpec((chunks, LANE, GDN_CHUNK), lambda i: (i, 0, 0)),
        ],
        out_shape=[
            jax.ShapeDtypeStruct((t, LANE), F32),
            jax.ShapeDtypeStruct((t // GDN_CHUNK, LANE, GDN_CHUNK), F32),
        ],
        compiler_params=_params("parallel"),
        name="gate_prep",
    )(zs, a_row, dt_row)


def _sgu_body(u_ref, v_ref, gate_ref, lg_ref, lb_ref, ws_ref, bs_ref, y_ref, *, chunks):
    c = SGU_CHUNK
    v = jax.nn.gelu(v_ref[0])
    mu = jnp.mean(v, axis=-1, keepdims=True)
    vc = v - mu
    var = jnp.mean(vc * vc, axis=-1, keepdims=True)
    vn = (vc * lax.rsqrt(var + EPS) * lg_ref[...] + lb_ref[...]).astype(BF16)
    for n in range(chunks):
        for g in range(SGU_GROUPS):
            rows, cols = slice(n * c, (n + 1) * c), slice(g * c, (g + 1) * c)
            mixed = jnp.dot(ws_ref[g], vn[rows, cols], preferred_element_type=F32) + bs_ref[g]
            gate = gate_ref[0, rows, cols]
            y = jax.nn.gelu(u_ref[0, rows, cols]) * mixed * (gate * jax.nn.sigmoid(gate))
            y_ref[0, rows, cols] = y.astype(y_ref.dtype)


def _sgu(z3, ln_g, ln_b, w_s, b_s, *, rows):
    b, s, _ = z3.shape
    rows = min(rows, s)
    chunks = rows // SGU_CHUNK
    col = lambda k: pl.BlockSpec((1, rows, W_A), lambda i, r, k=k: (i, r, k))
    return pl.pallas_call(
        functools.partial(_sgu_body, chunks=chunks),
        grid=(b, s // rows),
        in_specs=[
            col(0), col(1), col(2),
            pl.BlockSpec((1, W_A), lambda i, r: (0, 0)),
            pl.BlockSpec((1, W_A), lambda i, r: (0, 0)),
            pl.BlockSpec((SGU_GROUPS, SGU_CHUNK, SGU_CHUNK), lambda i, r: (0, 0, 0)),
            pl.BlockSpec((SGU_GROUPS, SGU_CHUNK, LANE), lambda i, r: (0, 0, 0)),
        ],
        out_specs=pl.BlockSpec((1, rows, W_A), lambda i, r: (i, r, 0)),
        out_shape=jax.ShapeDtypeStruct((b, s, W_A), BF16),
        compiler_params=_params("parallel", "parallel"),
        name="sgu",
    )(z3, z3, z3, ln_g.reshape(1, W_A), ln_b.reshape(1, W_A), w_s.astype(BF16),
      jnp.broadcast_to(b_s[:, :, None], (SGU_GROUPS, SGU_CHUNK, LANE)))


XA_PAIR = 2 * XA_HEAD_DIM


def _xattn_body(q_ref, gate_ref, k_ref, v_ref, y_ref):
    for hh in range(XA_PAIR // XA_HEAD_DIM):
        cols = slice(hh * XA_HEAD_DIM, (hh + 1) * XA_HEAD_DIM)
        q = q_ref[0, :, cols].astype(BF16)
        s = lax.dot_general(q, k_ref[0, :, cols], (((1,), (1,)), ((), ())), preferred_element_type=F32)
        s = s * (XA_HEAD_DIM ** -0.5)
        s = s - jnp.max(s, axis=-1, keepdims=True)
        p = jnp.exp(s)
        p = p / jnp.sum(p, axis=-1, keepdims=True)
        o = jnp.dot(p.astype(BF16), v_ref[0, :, cols], preferred_element_type=F32)
        gate = gate_ref[0, :, cols]
        y_ref[0, :, cols] = (o * (gate * jax.nn.sigmoid(gate))).astype(y_ref.dtype)


def _xattn(z3, kv3, *, rows):
    b, s, _ = z3.shape
    rows = min(rows, s)
    n_mem = kv3.shape[1]
    pairs = W_C // XA_PAIR
    q_blk = (3 * W_A + 4 * W_B) // XA_PAIR
    gate_blk = q_blk + pairs
    return pl.pallas_call(
        _xattn_body,
        grid=(b, s // rows, pairs),
        in_specs=[
            pl.BlockSpec((1, rows, XA_PAIR), lambda i, r, h: (i, r, q_blk + h)),
            pl.BlockSpec((1, rows, XA_PAIR), lambda i, r, h: (i, r, gate_blk + h)),
            pl.BlockSpec((1, n_mem, XA_PAIR), lambda i, r, h: (i, 0, h)),
            pl.BlockSpec((1, n_mem, XA_PAIR), lambda i, r, h: (i, 0, pairs + h)),
        ],
        out_specs=pl.BlockSpec((1, rows, XA_PAIR), lambda i, r, h: (i, r, h)),
        out_shape=jax.ShapeDtypeStruct((b, s, W_C), BF16),
        compiler_params=_params("parallel", "parallel", "parallel"),
        name="xattn",
    )(z3, z3, kv3, kv3)


def _gdn_masks():
    c = GDN_CHUNK
    ri = np.arange(c)[:, None]
    ci = np.arange(c)[None, :]
    x = ri ^ ci
    halves = []
    for strict, incl in ((ri > ci, ri >= ci), (ri < ci, ri <= ci)):
        halves.append(np.stack([
            np.where(incl, 0.0, MASKED_OUT),
            ri == ci,
            -1.0 * (strict & (x < 16)),
            strict & ((x >> 4) == 1),
            strict & ((x >> 5) == 1),
            strict & ((x >> 6) == 1),
        ]).astype(np.float32))
    masks = np.concatenate(halves, axis=2)
    return jnp.asarray(masks), jnp.asarray(masks[M_NEG_DIAG16:], dtype=BF16)


def _gdn_body(q_ref, k_ref, v_ref, gate_ref, g_ref, gt_ref, cq_ref, ck_ref, cv_ref, ng_ref, m_ref,
              mb_ref, y_ref,
              qn_s, kn_s, vn_s, qk0_s, kk_s,
              p_s, bd_s, off_s, qkd_s, rhs_s, kg_s, qg_s, uw_s,
              xq_s, c_s, op_s, eg_s, sbd_s, st_s, o_s, *, seq, n_items):
    c = GDN_CHUNK
    n_chunks = seq // c
    step = pl.program_id(0)
    h = jnp.minimum(step, n_items - 1) % GDN_HEADS
    lane = lax.broadcasted_iota(jnp.int32, (c, LANE), 1)
    lo, hi = slice(0, c), slice(c, 2 * c)
    halves = (lo, hi)

    @pl.when(step == 0)
    def _():
        for ref in (p_s, bd_s, off_s, qkd_s, rhs_s, kg_s, qg_s, uw_s, xq_s, c_s, op_s, eg_s, sbd_s,
                    st_s, o_s):
            ref[...] = jnp.zeros(ref.shape, ref.dtype)

    def after(token):
        bits = pltpu.bitcast(token, jnp.uint32)
        bits = lax.shift_right_logical(lax.shift_right_logical(bits, jnp.uint32(16)), jnp.uint32(16))
        return pltpu.bitcast(bits, F32)

    def conv_silu(x_ref, cw_ref, ci_, zero):
        halo = SUBLANE
        c0 = ci_ * c
        left = (CONV_K - 1) // 2
        cw = cw_ref[...]
        if zero is not None:
            cw = cw + zero[:CONV_K]
        if 0 < ci_ < n_chunks - 1:
            taps = [x_ref[0, c0 + j - left:c0 + j - left + c, :] for j in range(CONV_K)]
        else:
            zeros = jnp.zeros((halo, HEAD_DIM), F32)
            prev = x_ref[0, c0 - halo:c0, :] if ci_ > 0 else zeros
            nxt = x_ref[0, c0 + c:c0 + c + halo, :] if ci_ < n_chunks - 1 else zeros
            win = jnp.concatenate([prev, x_ref[0, c0:c0 + c, :], nxt], axis=0)
            taps = [win[halo + j - left:halo + j - left + c, :] for j in range(CONV_K)]
        acc = None
        for j in range(CONV_K):
            term = taps[j] * cw[j:j + 1, :]
            acc = term if acc is None else acc + term
        return acc * jax.nn.sigmoid(acc)

    def l2n(x, scale=1.0):
        return x * (lax.rsqrt(jnp.sum(x * x, axis=-1, keepdims=True) + EPS) * scale)

    def column(gc, idx):
        return jnp.sum(jnp.where(lane == idx, gc, 0.0), axis=1, keepdims=True)

    def gt_row(ci_, idx):
        return gt_ref[0, ci_, pl.ds(idx, 1), :]

    def set_bd(buf, j, x):
        buf[j, lo, lo] = x[:, lo]
        buf[j, hi, hi] = x[:, hi]

    def packed_lhs(buf, j):
        return jnp.concatenate([buf[j, lo, lo], buf[j, hi, hi]], axis=1)

    cached = set()

    def chunk_inputs(ci_, zero):
        if ci_ in cached:
            return qn_s[ci_], kn_s[ci_], vn_s[ci_], qk0_s[ci_], kk_s[ci_]
        cached.add(ci_)
        q = l2n(conv_silu(q_ref, cq_ref, ci_, zero), HEAD_DIM ** -0.5)
        k = l2n(conv_silu(k_ref, ck_ref, ci_, zero))
        v = conv_silu(v_ref, cv_ref, ci_, zero)
        kb = k.astype(BF16)
        qkk = lax.dot_general(jnp.concatenate([q.astype(BF16), kb], axis=0), kb,
                              (((1,), (1,)), ((), ())), preferred_element_type=F32)
        qn_s[ci_], kn_s[ci_], vn_s[ci_] = q, k, v
        qk0_s[ci_], kk_s[ci_] = qkk[:c], qkk[c:]
        return q, k, v, qkk[:c], qkk[c:]

    def build(s, token=None):
        zero = None if token is None else after(token)
        kd, qkd = [], []
        for d in range(2):
            ci_ = s if d == 0 else n_chunks - 1 - s
            q, k, v, qk0, kk = chunk_inputs(ci_, zero)
            gc = g_ref[0, ci_ * c:(ci_ + 1) * c, :]
            cum_r = gt_row(ci_, G_CUM + GDN_HEADS * d + h)
            tot_r = gt_row(ci_, G_TOT + GDN_HEADS * d + h)
            if zero is not None:
                gc = gc + jnp.concatenate([zero] * (c // SUBLANE), axis=0)
                cum_r, tot_r = cum_r + zero[:1], tot_r + zero[:1]
            beta = column(gc, G_BETA + GDN_HEADS * d + h)
            cum_c = column(gc, G_CUM + GDN_HEADS * d + h)
            e = jnp.exp(jnp.minimum(cum_c - cum_r, m_ref[M_BOUND, :, halves[d]]))
            e_c = jnp.exp(cum_c)
            kd.append(kk * beta * e)
            qkd.append(qk0 * e)
            rhs_s[s, d] = jnp.concatenate([v * beta, k * (beta * e_c)], axis=1).astype(BF16)
            kg_s[s, d] = (k * jnp.exp(tot_r - cum_c)).astype(BF16)
            qg_s[s, :, halves[d]] = q * e_c
            eg_s[s, :, halves[d]] = jnp.broadcast_to(jnp.exp(tot_r), (SUBLANE, c))
        kd = jnp.concatenate(kd, axis=1)
        qkd_s[s] = jnp.concatenate(qkd, axis=1).astype(BF16)
        p_s[s] = m_ref[M_EYE] + kd * m_ref[M_NEG_DIAG16]
        kd = kd.astype(BF16)
        set_bd(bd_s, s, kd * mb_ref[0])
        for lvl in range(3):
            off_s[s, lvl] = kd * mb_ref[1 + lvl]

    def neumann_first(j):
        n2 = jnp.dot(packed_lhs(bd_s, j), bd_s[j], preferred_element_type=F32)
        set_bd(bd_s, j, n2.astype(BF16))

    def neumann_step(j, last):
        n = packed_lhs(bd_s, j)
        rhs = bd_s[j]
        p = p_s[j]
        p_s[j] = p + jnp.dot(p.astype(BF16), rhs, preferred_element_type=F32)
        if not last:
            set_bd(bd_s, j, jnp.dot(n, rhs, preferred_element_type=F32).astype(BF16))

    def merge_a(j, lvl):
        set_bd(bd_s, j, p_s[j].astype(BF16))
        x = jnp.dot(off_s[j, lvl], bd_s[j], preferred_element_type=F32)
        set_bd(bd_s, j, x.astype(BF16))

    def merge_b(j):
        t = p_s[j]
        p_s[j] = t - jnp.dot(t.astype(BF16), bd_s[j], preferred_element_type=F32)

    def apply_t(j):
        t = p_s[j].astype(BF16)
        for d in range(2):
            uw = jnp.dot(t[:, halves[d]], rhs_s[j, d], preferred_element_type=F32)
            uw_s[j, d] = uw.astype(BF16)

    def finish(j):
        for d in range(2):
            uw = uw_s[j, d]
            cx = lax.dot_general(kg_s[j, d], uw, (((0,), (0,)), ((), ())), preferred_element_type=F32)
            ow = jnp.dot(qkd_s[j, :, halves[d]], uw, preferred_element_type=F32)
            c_s[j, :, halves[d]] = cx[:, lo]
            op_s[j, :, halves[d]] = ow[:, lo]
            xq_s[j, lo, halves[d]] = cx[:, hi].astype(BF16)
            xq_s[j, hi, halves[d]] = (qg_s[j, :, halves[d]] - ow[:, hi]).astype(BF16)
        return cx[:SUBLANE, lo]

    stages = [
        build,
        neumann_first,
        functools.partial(neumann_step, last=False),
        functools.partial(neumann_step, last=False),
        functools.partial(neumann_step, last=True),
        functools.partial(merge_a, lvl=0), merge_b,
        functools.partial(merge_a, lvl=1), merge_b,
        functools.partial(merge_a, lvl=2), merge_b,
        apply_t,
        finish,
    ]

    def scan_step(s):
        buf = s % 2
        state = st_s[...]
        set_bd(sbd_s, buf, state.astype(BF16))
        rhs = sbd_s[buf]
        xs = jnp.dot(xq_s[s, lo, :], rhs, preferred_element_type=F32)
        out = jnp.dot(xq_s[s, hi, :], rhs, preferred_element_type=F32) + op_s[s]
        o_s[0, s * c:(s + 1) * c, :] = out[:, lo]
        o_s[1, (n_chunks - 1 - s) * c:(n_chunks - s) * c, :] = out[:, hi]
        st_s[...] = state * eg_s[s, 0:1, :] - xs + c_s[s]

    def write_output():
        o = o_s[0] + o_s[1]
        o = o * lax.rsqrt(jnp.mean(o * o, axis=-1, keepdims=True) + EPS) * ng_ref[...]
        gate = gate_ref[0]
        y_ref[0] = (o * (gate * jax.nn.sigmoid(gate))).astype(y_ref.dtype)

    last = len(stages) - 1
    assert n_chunks >= last
    tokens = {}

    def run_stage(stage, s, t):
        if stage == 0:
            build(s, tokens.get(t - BUILD_LAG))
        elif stage == last:
            tokens[t] = finish(s)
        else:
            stages[stage](s)

    for t in range(n_chunks):
        for s in range(n_chunks):
            if 1 <= n_chunks + t - s <= last:
                run_stage(n_chunks + t - s, s, t)
        s_done = n_chunks + t - last
        if s_done < n_chunks:
            scan_step(s_done)
            if s_done == n_chunks - 1:
                write_output()
        for s in range(n_chunks):
            if 0 <= t - s <= last:
                run_stage(t - s, s, t)
        if t >= last:
            if t == last:
                st_s[...] = jnp.zeros(st_s.shape, F32)
            scan_step(t - last)


def _gdn(z3, g3, gt4, conv_w, norm_g):
    b, s, _ = z3.shape
    c = GDN_CHUNK
    n_chunks = s // c
    n_items = b * GDN_HEADS
    blk0 = 3 * W_A // HEAD_DIM
    started = lambda i: jnp.minimum(i, n_items - 1)
    finished = lambda i: jnp.maximum(i - 1, 0)
    head = lambda k: pl.BlockSpec(
        (1, s, HEAD_DIM),
        lambda i, k=k: (started(i) // GDN_HEADS, 0, blk0 + k * GDN_HEADS + started(i) % GDN_HEADS))
    cw = lambda k: pl.BlockSpec(
        (CONV_K, HEAD_DIM), lambda i, k=k: (0, k * GDN_HEADS + started(i) % GDN_HEADS))
    return pl.pallas_call(
        functools.partial(_gdn_body, seq=s, n_items=n_items),
        grid=(n_items + 1,),
        in_specs=[
            head(0), head(1), head(2),
            pl.BlockSpec((1, s, HEAD_DIM),
                         lambda i: (finished(i) // GDN_HEADS, 0,
                                    blk0 + 3 * GDN_HEADS + finished(i) % GDN_HEADS)),
            pl.BlockSpec((1, s, LANE), lambda i: (started(i) // GDN_HEADS, 0, 0)),
            pl.BlockSpec((1, n_chunks, LANE, c), lambda i: (started(i) // GDN_HEADS, 0, 0, 0)),
            cw(0), cw(1), cw(2),
            pl.BlockSpec((1, HEAD_DIM), lambda i: (0, 0)),
            pl.BlockSpec((6, c, 2 * c), lambda i: (0, 0, 0)),
            pl.BlockSpec((4, c, 2 * c), lambda i: (0, 0, 0)),
        ],
        out_specs=pl.BlockSpec((1, s, HEAD_DIM),
                               lambda i: (finished(i) // GDN_HEADS, 0, finished(i) % GDN_HEADS)),
        out_shape=jax.ShapeDtypeStruct((b, s, W_B), BF16),
        scratch_shapes=[
            pltpu.VMEM((n_chunks, c, HEAD_DIM), F32),
            pltpu.VMEM((n_chunks, c, HEAD_DIM), F32),
            pltpu.VMEM((n_chunks, c, HEAD_DIM), F32),
            pltpu.VMEM((n_chunks, c, c), F32),
            pltpu.VMEM((n_chunks, c, c), F32),
            pltpu.VMEM((n_chunks, c, 2 * c), F32),
            pltpu.VMEM((n_chunks, 2 * c, 2 * c), BF16),
            pltpu.VMEM((n_chunks, 3, c, 2 * c), BF16),
            pltpu.VMEM((n_chunks, c, 2 * c), BF16),
            pltpu.VMEM((n_chunks, 2, c, 2 * c), BF16),
            pltpu.VMEM((n_chunks, 2, c, HEAD_DIM), BF16),
            pltpu.VMEM((n_chunks, c, 2 * c), F32),
            pltpu.VMEM((n_chunks, 2, c, 2 * c), BF16),
            pltpu.VMEM((n_chunks, 2 * c, 2 * c), BF16),
            pltpu.VMEM((n_chunks, c, 2 * c), F32),
            pltpu.VMEM((n_chunks, c, 2 * c), F32),
            pltpu.VMEM((n_chunks, SUBLANE, 2 * c), F32),
            pltpu.VMEM((2, 2 * c, 2 * c), BF16),
            pltpu.VMEM((c, 2 * c), F32),
            pltpu.VMEM((2, s, HEAD_DIM), F32),
        ],
        compiler_params=_params("arbitrary"),
        name="gdn",
    )(z3, z3, z3, z3, g3, gt4, conv_w, conv_w, conv_w, norm_g.reshape(1, HEAD_DIM), *_gdn_masks())


def _out_proj_body(ya_ref, yb_ref, yc_ref, wa_ref, wb_ref, wc_ref, x_ref, *rest, final_norm):
    acc = jnp.dot(ya_ref[...], wa_ref[...], preferred_element_type=F32)
    acc += jnp.dot(yb_ref[...], wb_ref[...], preferred_element_type=F32)
    acc += jnp.dot(yc_ref[...], wc_ref[...], preferred_element_type=F32)
    if not final_norm:
        (o_ref,) = rest
        o_ref[...] = x_ref[...] + acc
        return
    g_ref, o_ref, row_s = rest
    j = pl.program_id(1)
    n_col, _, tn = row_s.shape
    row_s[j] = x_ref[...] + acc

    @pl.when(j == n_col - 1)
    def _():
        ss = None
        for jj in range(n_col):
            part = jnp.sum(row_s[jj] * row_s[jj], axis=-1, keepdims=True)
            ss = part if ss is None else ss + part
        scale = lax.rsqrt(ss / (n_col * tn) + EPS)
        for jj in range(n_col):
            cols = slice(jj * tn, (jj + 1) * tn)
            o_ref[:, cols] = row_s[jj] * scale * g_ref[:, cols]


def _out_proj(ya, yb, yc, w_out, x, final_g=None, *, tm, tn):
    t, d = x.shape
    tm = min(tm, t)
    final_norm = final_g is not None
    row = lambda w: pl.BlockSpec((tm, w), lambda i, j: (i, 0))
    wrows = lambda w, first: pl.BlockSpec((w, tn), lambda i, j: (first // w, j))
    in_specs = [row(W_A), row(W_B), row(W_C),
                wrows(W_A, 0), wrows(W_B, W_A), wrows(W_C, W_A + W_B),
                pl.BlockSpec((tm, tn), lambda i, j: (i, j))]
    assert W_A % W_B == 0 and (W_A + W_B) % W_C == 0
    args = [ya, yb, yc, w_out, w_out, w_out, x]
    if final_norm:
        in_specs.append(pl.BlockSpec((1, d), lambda i, j: (0, 0)))
        args.append(final_g.reshape(1, d))
        out_spec = pl.BlockSpec((tm, d), lambda i, j: (i, 0))
        scratch = [pltpu.VMEM((d // tn, tm, tn), F32)]
    else:
        out_spec = pl.BlockSpec((tm, tn), lambda i, j: (i, j))
        scratch = []
    return pl.pallas_call(
        functools.partial(_out_proj_body, final_norm=final_norm),
        grid=(t // tm, d // tn),
        in_specs=in_specs,
        out_specs=out_spec,
        out_shape=jax.ShapeDtypeStruct((t, d), F32),
        scratch_shapes=scratch,
        compiler_params=_params("parallel", "arbitrary" if final_norm else "parallel"),
        name="out_proj_norm" if final_norm else "out_proj",
    )(*args)


def _lane_row(values, offset):
    flat = values.reshape(-1).astype(F32)
    return jnp.zeros((1, LANE), F32).at[0, offset:offset + flat.shape[0]].set(flat)


def _prep_w_in_body(w_ref, o_ref):
    w = w_ref[0]
    o_ref[:, :OFF_SMALL] = w[:, :OFF_SMALL].astype(BF16)
    o_ref[:, OFF_SMALL:W_MAIN] = w[:, OFF_SMALL + N_SMALL:].astype(BF16)
    small = w[:, OFF_SMALL:OFF_SMALL + N_SMALL]
    pad = jnp.zeros((small.shape[0], LANE - N_SMALL), F32)
    o_ref[:, W_MAIN:] = jnp.concatenate([small, pad], axis=1).astype(BF16)


def _prep_w_in(w_in, layer, *, rows):
    _, d, n_in = w_in.shape
    return pl.pallas_call(
        _prep_w_in_body,
        grid=(d // rows,),
        in_specs=[pl.BlockSpec((1, rows, n_in), lambda i: (layer, i, 0))],
        out_specs=pl.BlockSpec((rows, W_MAIN + LANE), lambda i: (i, 0)),
        out_shape=jax.ShapeDtypeStruct((d, W_MAIN + LANE), BF16),
        compiler_params=_params("parallel"),
        name="prep_w_in",
    )(w_in)


def _layer_weights(w_in, w_mem_kv, w_out, layer):
    return (_prep_w_in(w_in, layer, rows=256), w_mem_kv[layer].astype(BF16),
            w_out[layer].astype(BF16))


def _layer(x, mem, weights, norm_g, sgu_ln_g, sgu_ln_b, sgu_w, sgu_b, conv_w, a_log, dt_bias,
           gdn_norm_g, mem_norm_g, final_g):
    b, s, d = x.shape
    t = b * s
    x2 = x.reshape(t, d)
    w_all, w_kv, w_out = weights
    z, zs = _norm_matmul(x2, norm_g, w_all, W_MAIN // LANE, tm=512, tn=1280, out_dtype=F32)
    z3 = z.reshape(b, s, W_MAIN)

    g, gt = _gate_prep(zs, _lane_row(a_log, G_CUM), _lane_row(dt_bias, G_CUM), rows=512)
    g3 = g.reshape(b, s, LANE)
    gt4 = gt.reshape(b, s // GDN_CHUNK, LANE, GDN_CHUNK)

    ya = _sgu(z3, sgu_ln_g, sgu_ln_b, sgu_w, sgu_b, rows=512)
    yb = _gdn(z3, g3, gt4, conv_w, gdn_norm_g)
    kv = _norm_matmul(mem.reshape(-1, d), mem_norm_g, w_kv, tm=512, tn=1024, out_dtype=BF16)
    yc = _xattn(z3, kv.reshape(b, -1, 2 * W_C), rows=1024)

    if final_g is None:
        out = _out_proj(ya.reshape(t, W_A), yb.reshape(t, W_B), yc.reshape(t, W_C), w_out, x2,
                        tm=1024, tn=512)
    else:
        out = _out_proj(ya.reshape(t, W_A), yb.reshape(t, W_B), yc.reshape(t, W_C), w_out, x2,
                        final_g, tm=512, tn=512)
    return out.reshape(b, s, d)


def kernel(x_prompt, x_sample, mem_prompt, mem_sample, norm_g, w_in, sgu_ln_g, sgu_ln_b, sgu_w,
           sgu_b, conv_w, a_log, dt_bias, gdn_norm_g, mem_norm_g, w_mem_kv, w_out, final_g):
    depth = norm_g.shape[0]
    weights = [_layer_weights(w_in, w_mem_kv, w_out, l) for l in range(depth)]

    def trunk(x, mem):
        for l in range(depth):
            x = _layer(x, mem, weights[l], norm_g[l], sgu_ln_g[l], sgu_ln_b[l], sgu_w[l], sgu_b[l],
                       conv_w[l], a_log[l], dt_bias[l], gdn_norm_g[l], mem_norm_g[l],
                       final_g if l == depth - 1 else None)
        return x

    return trunk(x_prompt, mem_prompt), trunk(x_sample, mem_sample)
```

```python
import functools

import jax
import jax.numpy as jnp
import numpy as np
from jax import lax
from jax.experimental import pallas as pl
from jax.experimental.pallas import tpu as pltpu

F32 = jnp.float32
BF16 = jnp.bfloat16
EPS = 1e-6

SGU_CHUNK = 128
SGU_GROUPS = 12
GDN_HEADS = 12
HEAD_DIM = 128
CONV_K = 5
XA_HEADS = 4
XA_HEAD_DIM = 256
W_A = SGU_GROUPS * SGU_CHUNK
W_B = GDN_HEADS * HEAD_DIM
W_C = XA_HEADS * XA_HEAD_DIM
W_MAIN = 3 * W_A + 4 * W_B + 2 * W_C
N_SMALL = 4 * GDN_HEADS
OFF_SMALL = 3 * W_A + 4 * W_B

LANE = 128
SUBLANE = 8
VMEM_LIMIT = 56 * 1024 * 1024

GDN_CHUNK = 128
BUILD_LAG = 2
G_BETA, G_CUM, G_TOT = 0, 2 * GDN_HEADS, 4 * GDN_HEADS
M_BOUND, M_EYE, M_NEG_DIAG16, M_OFF16, M_OFF32, M_OFF64 = range(6)
MASKED_OUT = -1e30


def _params(*sem):
    return pltpu.CompilerParams(dimension_semantics=sem, vmem_limit_bytes=VMEM_LIMIT)


def _norm_matmul_body(*refs, has_small):
    if has_small:
        x_ref, g_ref, w_ref, ws_ref, z_ref, zs_ref, h_ref = refs
    else:
        x_ref, g_ref, w_ref, z_ref, h_ref = refs

    @pl.when(pl.program_id(1) == 0)
    def _():
        x = x_ref[...]
        ms = jnp.mean(x * x, axis=-1, keepdims=True)
        h = (x * lax.rsqrt(ms + EPS) * g_ref[...]).astype(BF16)
        h_ref[...] = h
        if has_small:
            zs_ref[...] = jnp.dot(h, ws_ref[...], preferred_element_type=F32)

    z_ref[...] = jnp.dot(h_ref[...], w_ref[...], preferred_element_type=F32).astype(z_ref.dtype)


def _norm_matmul(x, g, w, small_block=None, *, tm, tn, out_dtype):
    t, d = x.shape
    has_small = small_block is not None
    n = small_block * LANE if has_small else w.shape[1]
    tm = min(tm, t)
    in_specs = [
        pl.BlockSpec((tm, d), lambda i, j: (i, 0)),
        pl.BlockSpec((1, d), lambda i, j: (0, 0)),
        pl.BlockSpec((d, tn), lambda i, j: (0, j)),
    ]
    out_shape = [jax.ShapeDtypeStruct((t, n), out_dtype)]
    out_specs = [pl.BlockSpec((tm, tn), lambda i, j: (i, j))]
    args = [x, g.reshape(1, d), w]
    if has_small:
        in_specs.append(pl.BlockSpec((d, LANE), lambda i, j: (0, small_block)))
        out_shape.append(jax.ShapeDtypeStruct((t, LANE), F32))
        out_specs.append(pl.BlockSpec((tm, LANE), lambda i, j: (i, 0)))
        args.append(w)
    outs = pl.pallas_call(
        functools.partial(_norm_matmul_body, has_small=has_small),
        grid=(t // tm, n // tn),
        in_specs=in_specs,
        out_specs=out_specs,
        out_shape=out_shape,
        scratch_shapes=[pltpu.VMEM((tm, d), BF16)],
        compiler_params=_params("parallel", "arbitrary"),
        name="norm_matmul_small" if has_small else "norm_matmul",
    )(*args)
    return outs if has_small else outs[0]


def _gate_prep_body(zs_ref, a_ref, dt_ref, g_ref, gt_ref, *, chunks):
    c = GDN_CHUNK
    lane = lax.broadcasted_iota(jnp.int32, (c, LANE), 1)
    ri = lax.broadcasted_iota(jnp.int32, (c, c), 0)
    ci = lax.broadcasted_iota(jnp.int32, (c, c), 1)
    lower = (ri >= ci).astype(F32)
    upper = (ri <= ci).astype(F32)
    is_decay = (lane >= G_CUM) & (lane < G_TOT)
    neg_a = -jnp.exp(a_ref[...])
    for n in range(chunks):
        zs = zs_ref[pl.ds(n * c, c), :]
        beta = jax.nn.sigmoid(zs)
        g = jnp.where(is_decay, neg_a * jax.nn.softplus(zs + dt_ref[...]), 0.0)
        cum_fw = jnp.dot(lower, g, preferred_element_type=F32, precision=lax.Precision.HIGHEST)
        cum_bw = jnp.dot(upper, g, preferred_element_type=F32, precision=lax.Precision.HIGHEST)
        tot = jnp.broadcast_to(jnp.sum(g, axis=0, keepdims=True), (c, LANE))
        tot = pltpu.roll(tot, G_TOT - G_CUM, 1)
        out = jnp.where(lane < G_CUM, beta,
                        jnp.where(lane < G_CUM + GDN_HEADS, cum_fw,
                                  jnp.where(lane < G_TOT, cum_bw,
                                            jnp.where(lane < G_TOT + 2 * GDN_HEADS, tot, 0.0))))
        g_ref[pl.ds(n * c, c), :] = out
        gt_ref[n] = out.T


def _gate_prep(zs, a_row, dt_row, *, rows):
    t = zs.shape[0]
    rows = min(rows, t)
    chunks = rows // GDN_CHUNK
    return pl.pallas_call(
        functools.partial(_gate_prep_body, chunks=chunks),
        grid=(t // rows,),
        in_specs=[
            pl.BlockSpec((rows, LANE), lambda i: (i, 0)),
            pl.BlockSpec((1, LANE), lambda i: (0, 0)),
            pl.BlockSpec((1, LANE), lambda i: (0, 0)),
        ],
        out_specs=[
            pl.BlockSpec((rows, LANE), lambda i: (i, 0)),
            pl.BlockSpec((chunks, LANE, GDN_CHUNK), lambda i: (i, 0, 0)),
        ],
        out_shape=[
            jax.ShapeDtypeStruct((t, LANE), F32),
            jax.ShapeDtypeStruct((t // GDN_CHUNK, LANE, GDN_CHUNK), F32),
        ],
        compiler_params=_params("parallel"),
        name="gate_prep",
    )(zs, a_row, dt_row)


def _sgu_body(u_ref, v_ref, gate_ref, lg_ref, lb_ref, ws_ref, bs_ref, y_ref, *, chunks):
    c = SGU_CHUNK
    v = jax.nn.gelu(v_ref[0])
    mu = jnp.mean(v, axis=-1, keepdims=True)
    vc = v - mu
    var = jnp.mean(vc * vc, axis=-1, keepdims=True)
    vn = (vc * lax.rsqrt(var + EPS) * lg_ref[...] + lb_ref[...]).astype(BF16)
    for n in range(chunks):
        for g in range(SGU_GROUPS):
            rows, cols = slice(n * c, (n + 1) * c), slice(g * c, (g + 1) * c)
            mixed = jnp.dot(ws_ref[g], vn[rows, cols], preferred_element_type=F32) + bs_ref[g]
            gate = gate_ref[0, rows, cols]
            y = jax.nn.gelu(u_ref[0, rows, cols]) * mixed * (gate * jax.nn.sigmoid(gate))
            y_ref[0, rows, cols] = y.astype(y_ref.dtype)


def _sgu(z3, ln_g, ln_b, w_s, b_s, *, rows):
    b, s, _ = z3.shape
    rows = min(rows, s)
    chunks = rows // SGU_CHUNK
    col = lambda k: pl.BlockSpec((1, rows, W_A), lambda i, r, k=k: (i, r, k))
    return pl.pallas_call(
        functools.partial(_sgu_body, chunks=chunks),
        grid=(b, s // rows),
        in_specs=[
            col(0), col(1), col(2),
            pl.BlockSpec((1, W_A), lambda i, r: (0, 0)),
            pl.BlockSpec((1, W_A), lambda i, r: (0, 0)),
            pl.BlockSpec((SGU_GROUPS, SGU_CHUNK, SGU_CHUNK), lambda i, r: (0, 0, 0)),
            pl.BlockSpec((SGU_GROUPS, SGU_CHUNK, LANE), lambda i, r: (0, 0, 0)),
        ],
        out_specs=pl.BlockSpec((1, rows, W_A), lambda i, r: (i, r, 0)),
        out_shape=jax.ShapeDtypeStruct((b, s, W_A), BF16),
        compiler_params=_params("parallel", "parallel"),
        name="sgu",
    )(z3, z3, z3, ln_g.reshape(1, W_A), ln_b.reshape(1, W_A), w_s.astype(BF16),
      jnp.broadcast_to(b_s[:, :, None], (SGU_GROUPS, SGU_CHUNK, LANE)))


XA_PAIR = 2 * XA_HEAD_DIM


def _xattn_body(q_ref, gate_ref, k_ref, v_ref, y_ref):
    for hh in range(XA_PAIR // XA_HEAD_DIM):
        cols = slice(hh * XA_HEAD_DIM, (hh + 1) * XA_HEAD_DIM)
        q = q_ref[0, :, cols].astype(BF16)
        s = lax.dot_general(q, k_ref[0, :, cols], (((1,), (1,)), ((), ())), preferred_element_type=F32)
        s = s * (XA_HEAD_DIM ** -0.5)
        s = s - jnp.max(s, axis=-1, keepdims=True)
        p = jnp.exp(s)
        p = p / jnp.sum(p, axis=-1, keepdims=True)
        o = jnp.dot(p.astype(BF16), v_ref[0, :, cols], preferred_element_type=F32)
        gate = gate_ref[0, :, cols]
        y_ref[0, :, cols] = (o * (gate * jax.nn.sigmoid(gate))).astype(y_ref.dtype)


def _xattn(z3, kv3, *, rows):
    b, s, _ = z3.shape
    rows = min(rows, s)
    n_mem = kv3.shape[1]
    pairs = W_C // XA_PAIR
    q_blk = (3 * W_A + 4 * W_B) // XA_PAIR
    gate_blk = q_blk + pairs
    return pl.pallas_call(
        _xattn_body,
        grid=(b, s // rows, pairs),
        in_specs=[
            pl.BlockSpec((1, rows, XA_PAIR), lambda i, r, h: (i, r, q_blk + h)),
            pl.BlockSpec((1, rows, XA_PAIR), lambda i, r, h: (i, r, gate_blk + h)),
            pl.BlockSpec((1, n_mem, XA_PAIR), lambda i, r, h: (i, 0, h)),
            pl.BlockSpec((1, n_mem, XA_PAIR), lambda i, r, h: (i, 0, pairs + h)),
        ],
        out_specs=pl.BlockSpec((1, rows, XA_PAIR), lambda i, r, h: (i, r, h)),
        out_shape=jax.ShapeDtypeStruct((b, s, W_C), BF16),
        compiler_params=_params("parallel", "parallel", "parallel"),
        name="xattn",
    )(z3, z3, kv3, kv3)


def _gdn_masks():
    c = GDN_CHUNK
    ri = np.arange(c)[:, None]
    ci = np.arange(c)[None, :]
    x = ri ^ ci
    halves = []
    for strict, incl in ((ri > ci, ri >= ci), (ri < ci, ri <= ci)):
        halves.append(np.stack([
            np.where(incl, 0.0, MASKED_OUT),
            ri == ci,
            -1.0 * (strict & (x < 16)),
            strict & ((x >> 4) == 1),
            strict & ((x >> 5) == 1),
            strict & ((x >> 6) == 1),
        ]).astype(np.float32))
    masks = np.concatenate(halves, axis=2)
    return jnp.asarray(masks), jnp.asarray(masks[M_NEG_DIAG16:], dtype=BF16)


def _gdn_body(q_ref, k_ref, v_ref, gate_ref, g_ref, gt_ref, cq_ref, ck_ref, cv_ref, ng_ref, m_ref,
              mb_ref, y_ref,
              qn_s, kn_s, vn_s, qk0_s, kk_s,
              p_s, bd_s, off_s, qkd_s, rhs_s, kg_s, qg_s, uw_s,
              xq_s, c_s, op_s, eg_s, sbd_s, st_s, o_s, *, seq, n_items):
    c = GDN_CHUNK
    n_chunks = seq // c
    step = pl.program_id(0)
    h = jnp.minimum(step, n_items - 1) % GDN_HEADS
    lane = lax.broadcasted_iota(jnp.int32, (c, LANE), 1)
    lo, hi = slice(0, c), slice(c, 2 * c)
    halves = (lo, hi)

    @pl.when(step == 0)
    def _():
        for ref in (p_s, bd_s, off_s, qkd_s, rhs_s, kg_s, qg_s, uw_s, xq_s, c_s, op_s, eg_s, sbd_s,
                    st_s, o_s):
            ref[...] = jnp.zeros(ref.shape, ref.dtype)

    def after(token):
        bits = pltpu.bitcast(token, jnp.uint32)
        bits = lax.shift_right_logical(lax.shift_right_logical(bits, jnp.uint32(16)), jnp.uint32(16))
        return pltpu.bitcast(bits, F32)

    def conv_silu(x_ref, cw_ref, ci_, zero):
        halo = SUBLANE
        c0 = ci_ * c
        left = (CONV_K - 1) // 2
        cw = cw_ref[...]
        if zero is not None:
            cw = cw + zero[:CONV_K]
        if 0 < ci_ < n_chunks - 1:
            taps = [x_ref[0, c0 + j - left:c0 + j - left + c, :] for j in range(CONV_K)]
        else:
            zeros = jnp.zeros((halo, HEAD_DIM), F32)
            prev = x_ref[0, c0 - halo:c0, :] if ci_ > 0 else zeros
            nxt = x_ref[0, c0 + c:c0 + c + halo, :] if ci_ < n_chunks - 1 else zeros
            win = jnp.concatenate([prev, x_ref[0, c0:c0 + c, :], nxt], axis=0)
            taps = [win[halo + j - left:halo + j - left + c, :] for j in range(CONV_K)]
        acc = None
        for j in range(CONV_K):
            term = taps[j] * cw[j:j + 1, :]
            acc = term if acc is None else acc + term
        return acc * jax.nn.sigmoid(acc)

    def l2n(x, scale=1.0):
        return x * (lax.rsqrt(jnp.sum(x * x, axis=-1, keepdims=True) + EPS) * scale)

    def column(gc, idx):
        return jnp.sum(jnp.where(lane == idx, gc, 0.0), axis=1, keepdims=True)

    def gt_row(ci_, idx):
        return gt_ref[0, ci_, pl.ds(idx, 1), :]

    def set_bd(buf, j, x):
        buf[j, lo, lo] = x[:, lo]
        buf[j, hi, hi] = x[:, hi]

    def packed_lhs(buf, j):
        return jnp.concatenate([buf[j, lo, lo], buf[j, hi, hi]], axis=1)

    cached = set()

    def chunk_inputs(ci_, zero):
        if ci_ in cached:
            return qn_s[ci_], kn_s[ci_], vn_s[ci_], qk0_s[ci_], kk_s[ci_]
        cached.add(ci_)
        q = l2n(conv_silu(q_ref, cq_ref, ci_, zero), HEAD_DIM ** -0.5)
        k = l2n(conv_silu(k_ref, ck_ref, ci_, zero))
        v = conv_silu(v_ref, cv_ref, ci_, zero)
        kb = k.astype(BF16)
        qkk = lax.dot_general(jnp.concatenate([q.astype(BF16), kb], axis=0), kb,
                              (((1,), (1,)), ((), ())), preferred_element_type=F32)
        qn_s[ci_], kn_s[ci_], vn_s[ci_] = q, k, v
        qk0_s[ci_], kk_s[ci_] = qkk[:c], qkk[c:]
        return q, k, v, qkk[:c], qkk[c:]

    def build(s, token=None):
        zero = None if token is None else after(token)
        kd, qkd = [], []
        for d in range(2):
            ci_ = s if d == 0 else n_chunks - 1 - s
            q, k, v, qk0, kk = chunk_inputs(ci_, zero)
            gc = g_ref[0, ci_ * c:(ci_ + 1) * c, :]
            cum_r = gt_row(ci_, G_CUM + GDN_HEADS * d + h)
            tot_r = gt_row(ci_, G_TOT + GDN_HEADS * d + h)
            if zero is not None:
                gc = gc + jnp.concatenate([zero] * (c // SUBLANE), axis=0)
                cum_r, tot_r = cum_r + zero[:1], tot_r + zero[:1]
            beta = column(gc, G_BETA + GDN_HEADS * d + h)
            cum_c = column(gc, G_CUM + GDN_HEADS * d + h)
            e = jnp.exp(jnp.minimum(cum_c - cum_r, m_ref[M_BOUND, :, halves[d]]))
            e_c = jnp.exp(cum_c)
            kd.append(kk * beta * e)
            qkd.append(qk0 * e)
            rhs_s[s, d] = jnp.concatenate([v * beta, k * (beta * e_c)], axis=1).astype(BF16)
            kg_s[s, d] = (k * jnp.exp(tot_r - cum_c)).astype(BF16)
            qg_s[s, :, halves[d]] = q * e_c
            eg_s[s, :, halves[d]] = jnp.broadcast_to(jnp.exp(tot_r), (SUBLANE, c))
        kd = jnp.concatenate(kd, axis=1)
        qkd_s[s] = jnp.concatenate(qkd, axis=1).astype(BF16)
        p_s[s] = m_ref[M_EYE] + kd * m_ref[M_NEG_DIAG16]
        kd = kd.astype(BF16)
        set_bd(bd_s, s, kd * mb_ref[0])
        for lvl in range(3):
            off_s[s, lvl] = kd * mb_ref[1 + lvl]

    def neumann_first(j):
        n2 = jnp.dot(packed_lhs(bd_s, j), bd_s[j], preferred_element_type=F32)
        set_bd(bd_s, j, n2.astype(BF16))

    def neumann_step(j, last):
        n = packed_lhs(bd_s, j)
        rhs = bd_s[j]
        p = p_s[j]
        p_s[j] = p + jnp.dot(p.astype(BF16), rhs, preferred_element_type=F32)
        if not last:
            set_bd(bd_s, j, jnp.dot(n, rhs, preferred_element_type=F32).astype(BF16))

    def merge_a(j, lvl):
        set_bd(bd_s, j, p_s[j].astype(BF16))
        x = jnp.dot(off_s[j, lvl], bd_s[j], preferred_element_type=F32)
        set_bd(bd_s, j, x.astype(BF16))

    def merge_b(j):
        t = p_s[j]
        p_s[j] = t - jnp.dot(t.astype(BF16), bd_s[j], preferred_element_type=F32)

    def apply_t(j):
        t = p_s[j].astype(BF16)
        for d in range(2):
            uw = jnp.dot(t[:, halves[d]], rhs_s[j, d], preferred_element_type=F32)
            uw_s[j, d] = uw.astype(BF16)

    def finish(j):
        for d in range(2):
            uw = uw_s[j, d]
            cx = lax.dot_general(kg_s[j, d], uw, (((0,), (0,)), ((), ())), preferred_element_type=F32)
            ow = jnp.dot(qkd_s[j, :, halves[d]], uw, preferred_element_type=F32)
            c_s[j, :, halves[d]] = cx[:, lo]
            op_s[j, :, halves[d]] = ow[:, lo]
            xq_s[j, lo, halves[d]] = cx[:, hi].astype(BF16)
            xq_s[j, hi, halves[d]] = (qg_s[j, :, halves[d]] - ow[:, hi]).astype(BF16)
        return cx[:SUBLANE, lo]

    stages = [
        build,
        neumann_first,
        functools.partial(neumann_step, last=False),
        functools.partial(neumann_step, last=False),
        functools.partial(neumann_step, last=True),
        functools.partial(merge_a, lvl=0), merge_b,
        functools.partial(merge_a, lvl=1), merge_b,
        functools.partial(merge_a, lvl=2), merge_b,
        apply_t,
        finish,
    ]

    def scan_step(s):
        buf = s % 2
        state = st_s[...]
        set_bd(sbd_s, buf, state.astype(BF16))
        rhs = sbd_s[buf]
        xs = jnp.dot(xq_s[s, lo, :], rhs, preferred_element_type=F32)
        out = jnp.dot(xq_s[s, hi, :], rhs, preferred_element_type=F32) + op_s[s]
        o_s[0, s * c:(s + 1) * c, :] = out[:, lo]
        o_s[1, (n_chunks - 1 - s) * c:(n_chunks - s) * c, :] = out[:, hi]
        st_s[...] = state * eg_s[s, 0:1, :] - xs + c_s[s]

    def write_output():
        o = o_s[0] + o_s[1]
        o = o * lax.rsqrt(jnp.mean(o * o, axis=-1, keepdims=True) + EPS) * ng_ref[...]
        gate = gate_ref[0]
        y_ref[0] = (o * (gate * jax.nn.sigmoid(gate))).astype(y_ref.dtype)

    last = len(stages) - 1
    assert n_chunks >= last
    tokens = {}

    def run_stage(stage, s, t):
        if stage == 0:
            build(s, tokens.get(t - BUILD_LAG))
        elif stage == last:
            tokens[t] = finish(s)
        else:
            stages[stage](s)

    for t in range(n_chunks):
        for s in range(n_chunks):
            if 1 <= n_chunks + t - s <= last:
                run_stage(n_chunks + t - s, s, t)
        s_done = n_chunks + t - last
        if s_done < n_chunks:
            scan_step(s_done)
            if s_done == n_chunks - 1:
                write_output()
        for s in range(n_chunks):
            if 0 <= t - s <= last:
                run_stage(t - s, s, t)
        if t >= last:
            if t == last:
                st_s[...] = jnp.zeros(st_s.shape, F32)
            scan_step(t - last)


def _gdn(z3, g3, gt4, conv_w, norm_g):
    b, s, _ = z3.shape
    c = GDN_CHUNK
    n_chunks = s // c
    n_items = b * GDN_HEADS
    blk0 = 3 * W_A // HEAD_DIM
    started = lambda i: jnp.minimum(i, n_items - 1)
    finished = lambda i: jnp.maximum(i - 1, 0)
    head = lambda k: pl.BlockSpec(
        (1, s, HEAD_DIM),
        lambda i, k=k: (started(i) // GDN_HEADS, 0, blk0 + k * GDN_HEADS + started(i) % GDN_HEADS))
    cw = lambda k: pl.BlockSpec(
        (CONV_K, HEAD_DIM), lambda i, k=k: (0, k * GDN_HEADS + started(i) % GDN_HEADS))
    return pl.pallas_call(
        functools.partial(_gdn_body, seq=s, n_items=n_items),
        grid=(n_items + 1,),
        in_specs=[
            head(0), head(1), head(2),
            pl.BlockSpec((1, s, HEAD_DIM),
                         lambda i: (finished(i) // GDN_HEADS, 0,
                                    blk0 + 3 * GDN_HEADS + finished(i) % GDN_HEADS)),
            pl.BlockSpec((1, s, LANE), lambda i: (started(i) // GDN_HEADS, 0, 0)),
            pl.BlockSpec((1, n_chunks, LANE, c), lambda i: (started(i) // GDN_HEADS, 0, 0, 0)),
            cw(0), cw(1), cw(2),
            pl.BlockSpec((1, HEAD_DIM), lambda i: (0, 0)),
            pl.BlockSpec((6, c, 2 * c), lambda i: (0, 0, 0)),
            pl.BlockSpec((4, c, 2 * c), lambda i: (0, 0, 0)),
        ],
        out_specs=pl.BlockSpec((1, s, HEAD_DIM),
                               lambda i: (finished(i) // GDN_HEADS, 0, finished(i) % GDN_HEADS)),
        out_shape=jax.ShapeDtypeStruct((b, s, W_B), BF16),
        scratch_shapes=[
            pltpu.VMEM((n_chunks, c, HEAD_DIM), F32),
            pltpu.VMEM((n_chunks, c, HEAD_DIM), F32),
            pltpu.VMEM((n_chunks, c, HEAD_DIM), F32),
            pltpu.VMEM((n_chunks, c, c), F32),
            pltpu.VMEM((n_chunks, c, c), F32),
            pltpu.VMEM((n_chunks, c, 2 * c), F32),
            pltpu.VMEM((n_chunks, 2 * c, 2 * c), BF16),
            pltpu.VMEM((n_chunks, 3, c, 2 * c), BF16),
            pltpu.VMEM((n_chunks, c, 2 * c), BF16),
            pltpu.VMEM((n_chunks, 2, c, 2 * c), BF16),
            pltpu.VMEM((n_chunks, 2, c, HEAD_DIM), BF16),
            pltpu.VMEM((n_chunks, c, 2 * c), F32),
            pltpu.VMEM((n_chunks, 2, c, 2 * c), BF16),
            pltpu.VMEM((n_chunks, 2 * c, 2 * c), BF16),
            pltpu.VMEM((n_chunks, c, 2 * c), F32),
            pltpu.VMEM((n_chunks, c, 2 * c), F32),
            pltpu.VMEM((n_chunks, SUBLANE, 2 * c), F32),
            pltpu.VMEM((2, 2 * c, 2 * c), BF16),
            pltpu.VMEM((c, 2 * c), F32),
            pltpu.VMEM((2, s, HEAD_DIM), F32),
        ],
        compiler_params=_params("arbitrary"),
        name="gdn",
    )(z3, z3, z3, z3, g3, gt4, conv_w, conv_w, conv_w, norm_g.reshape(1, HEAD_DIM), *_gdn_masks())


def _out_proj_body(ya_ref, yb_ref, yc_ref, wa_ref, wb_ref, wc_ref, x_ref, *rest, final_norm):
    acc = jnp.dot(ya_ref[...], wa_ref[...], preferred_element_type=F32)
    acc += jnp.dot(yb_ref[...], wb_ref[...], preferred_element_type=F32)
    acc += jnp.dot(yc_ref[...], wc_ref[...], preferred_element_type=F32)
    if not final_norm:
        (o_ref,) = rest
        o_ref[...] = x_ref[...] + acc
        return
    g_ref, o_ref, row_s = rest
    j = pl.program_id(1)
    n_col, _, tn = row_s.shape
    row_s[j] = x_ref[...] + acc

    @pl.when(j == n_col - 1)
    def _():
        ss = None
        for jj in range(n_col):
            part = jnp.sum(row_s[jj] * row_s[jj], axis=-1, keepdims=True)
            ss = part if ss is None else ss + part
        scale = lax.rsqrt(ss / (n_col * tn) + EPS)
        for jj in range(n_col):
            cols = slice(jj * tn, (jj + 1) * tn)
            o_ref[:, cols] = row_s[jj] * scale * g_ref[:, cols]


def _out_proj(ya, yb, yc, w_out, x, final_g=None, *, tm, tn):
    t, d = x.shape
    tm = min(tm, t)
    final_norm = final_g is not None
    row = lambda w: pl.BlockSpec((tm, w), lambda i, j: (i, 0))
    wrows = lambda w, first: pl.BlockSpec((w, tn), lambda i, j: (first // w, j))
    in_specs = [row(W_A), row(W_B), row(W_C),
                wrows(W_A, 0), wrows(W_B, W_A), wrows(W_C, W_A + W_B),
                pl.BlockSpec((tm, tn), lambda i, j: (i, j))]
    assert W_A % W_B == 0 and (W_A + W_B) % W_C == 0
    args = [ya, yb, yc, w_out, w_out, w_out, x]
    if final_norm:
        in_specs.append(pl.BlockSpec((1, d), lambda i, j: (0, 0)))
        args.append(final_g.reshape(1, d))
        out_spec = pl.BlockSpec((tm, d), lambda i, j: (i, 0))
        scratch = [pltpu.VMEM((d // tn, tm, tn), F32)]
    else:
        out_spec = pl.BlockSpec((tm, tn), lambda i, j: (i, j))
        scratch = []
    return pl.pallas_call(
        functools.partial(_out_proj_body, final_norm=final_norm),
        grid=(t // tm, d // tn),
        in_specs=in_specs,
        out_specs=out_spec,
        out_shape=jax.ShapeDtypeStruct((t, d), F32),
        scratch_shapes=scratch,
        compiler_params=_params("parallel", "arbitrary" if final_norm else "parallel"),
        name="out_proj_norm" if final_norm else "out_proj",
    )(*args)


def _lane_row(values, offset):
    flat = values.reshape(-1).astype(F32)
    return jnp.zeros((1, LANE), F32).at[0, offset:offset + flat.shape[0]].set(flat)


def _layer_weights(w_in, w_mem_kv, w_out):
    d = w_in.shape[0]
    w_all = jnp.concatenate(
        [w_in[:, :OFF_SMALL], w_in[:, OFF_SMALL + N_SMALL:], w_in[:, OFF_SMALL:OFF_SMALL + N_SMALL],
         jnp.zeros((d, LANE - N_SMALL), w_in.dtype)], axis=1).astype(BF16)
    return w_all, w_mem_kv.astype(BF16), w_out.astype(BF16)


def _layer(x, mem, weights, norm_g, sgu_ln_g, sgu_ln_b, sgu_w, sgu_b, conv_w, a_log, dt_bias,
           gdn_norm_g, mem_norm_g, final_g):
    b, s, d = x.shape
    t = b * s
    x2 = x.reshape(t, d)
    w_all, w_kv, w_out = weights
    z, zs = _norm_matmul(x2, norm_g, w_all, W_MAIN // LANE, tm=512, tn=1280, out_dtype=F32)
    z3 = z.reshape(b, s, W_MAIN)

    g, gt = _gate_prep(zs, _lane_row(a_log, G_CUM), _lane_row(dt_bias, G_CUM), rows=512)
    g3 = g.reshape(b, s, LANE)
    gt4 = gt.reshape(b, s // GDN_CHUNK, LANE, GDN_CHUNK)

    ya = _sgu(z3, sgu_ln_g, sgu_ln_b, sgu_w, sgu_b, rows=512)
    yb = _gdn(z3, g3, gt4, conv_w, gdn_norm_g)
    kv = _norm_matmul(mem.reshape(-1, d), mem_norm_g, w_kv, tm=512, tn=1024, out_dtype=BF16)
    yc = _xattn(z3, kv.reshape(b, -1, 2 * W_C), rows=1024)

    if final_g is None:
        out = _out_proj(ya.reshape(t, W_A), yb.reshape(t, W_B), yc.reshape(t, W_C), w_out, x2,
                        tm=1024, tn=512)
    else:
        out = _out_proj(ya.reshape(t, W_A), yb.reshape(t, W_B), yc.reshape(t, W_C), w_out, x2,
                        final_g, tm=512, tn=512)
    return out.reshape(b, s, d)


def kernel(x_prompt, x_sample, mem_prompt, mem_sample, norm_g, w_in, sgu_ln_g, sgu_ln_b, sgu_w,
           sgu_b, conv_w, a_log, dt_bias, gdn_norm_g, mem_norm_g, w_mem_kv, w_out, final_g):
    depth = norm_g.shape[0]
    weights = [_layer_weights(w_in[l], w_mem_kv[l], w_out[l]) for l in range(depth)]

    def trunk(x, mem):
        for l in range(depth):
            x = _layer(x, mem, weights[l], norm_g[l], sgu_ln_g[l], sgu_ln_b[l], sgu_w[l], sgu_b[l],
                       conv_w[l], a_log[l], dt_bias[l], gdn_norm_g[l], mem_norm_g[l],
                       final_g if l == depth - 1 else None)
        return x

    return trunk(x_prompt, mem_prompt), trunk(x_sample, mem_sample)
```

```python
import functools

import jax
import jax.numpy as jnp
import numpy as np
from jax import lax
from jax.experimental import pallas as pl
from jax.experimental.pallas import tpu as pltpu

F32 = jnp.float32
BF16 = jnp.bfloat16
EPS = 1e-6

SGU_CHUNK = 128
SGU_GROUPS = 12
GDN_HEADS = 12
HEAD_DIM = 128
CONV_K = 5
XA_HEADS = 4
XA_HEAD_DIM = 256
W_A = SGU_GROUPS * SGU_CHUNK
W_B = GDN_HEADS * HEAD_DIM
W_C = XA_HEADS * XA_HEAD_DIM
W_MAIN = 3 * W_A + 4 * W_B + 2 * W_C
N_SMALL = 4 * GDN_HEADS
OFF_SMALL = 3 * W_A + 4 * W_B

LANE = 128
SUBLANE = 8
VMEM_LIMIT = 56 * 1024 * 1024

GDN_CHUNK = 128
BUILD_LAG = 2
G_BETA, G_CUM, G_TOT = 0, 2 * GDN_HEADS, 4 * GDN_HEADS
M_BOUND, M_EYE, M_NEG_DIAG16, M_OFF16, M_OFF32, M_OFF64 = range(6)
MASKED_OUT = -1e30


def _params(*sem):
    return pltpu.CompilerParams(dimension_semantics=sem, vmem_limit_bytes=VMEM_LIMIT)


def _norm_matmul_body(*refs, has_small):
    if has_small:
        x_ref, g_ref, w_ref, ws_ref, z_ref, zs_ref, h_ref = refs
    else:
        x_ref, g_ref, w_ref, z_ref, h_ref = refs

    @pl.when(pl.program_id(1) == 0)
    def _():
        x = x_ref[...]
        ms = jnp.mean(x * x, axis=-1, keepdims=True)
        h = (x * lax.rsqrt(ms + EPS) * g_ref[...]).astype(BF16)
        h_ref[...] = h
        if has_small:
            zs_ref[...] = jnp.dot(h, ws_ref[...], preferred_element_type=F32)

    z_ref[...] = jnp.dot(h_ref[...], w_ref[...], preferred_element_type=F32).astype(z_ref.dtype)


def _norm_matmul(x, g, w, small_block=None, *, tm, tn, out_dtype):
    t, d = x.shape
    has_small = small_block is not None
    n = small_block * LANE if has_small else w.shape[1]
    tm = min(tm, t)
    in_specs = [
        pl.BlockSpec((tm, d), lambda i, j: (i, 0)),
        pl.BlockSpec((1, d), lambda i, j: (0, 0)),
        pl.BlockSpec((d, tn), lambda i, j: (0, j)),
    ]
    out_shape = [jax.ShapeDtypeStruct((t, n), out_dtype)]
    out_specs = [pl.BlockSpec((tm, tn), lambda i, j: (i, j))]
    args = [x, g.reshape(1, d), w]
    if has_small:
        in_specs.append(pl.BlockSpec((d, LANE), lambda i, j: (0, small_block)))
        out_shape.append(jax.ShapeDtypeStruct((t, LANE), F32))
        out_specs.append(pl.BlockSpec((tm, LANE), lambda i, j: (i, 0)))
        args.append(w)
    outs = pl.pallas_call(
        functools.partial(_norm_matmul_body, has_small=has_small),
        grid=(t // tm, n // tn),
        in_specs=in_specs,
        out_specs=out_specs,
        out_shape=out_shape,
        scratch_shapes=[pltpu.VMEM((tm, d), BF16)],
        compiler_params=_params("parallel", "arbitrary"),
        name="norm_matmul_small" if has_small else "norm_matmul",
    )(*args)
    return outs if has_small else outs[0]


def _gate_prep_body(zs_ref, a_ref, dt_ref, g_ref, gt_ref, *, chunks):
    c = GDN_CHUNK
    lane = lax.broadcasted_iota(jnp.int32, (c, LANE), 1)
    ri = lax.broadcasted_iota(jnp.int32, (c, c), 0)
    ci = lax.broadcasted_iota(jnp.int32, (c, c), 1)
    lower = (ri >= ci).astype(F32)
    upper = (ri <= ci).astype(F32)
    is_decay = (lane >= G_CUM) & (lane < G_TOT)
    neg_a = -jnp.exp(a_ref[...])
    for n in range(chunks):
        zs = zs_ref[pl.ds(n * c, c), :]
        beta = jax.nn.sigmoid(zs)
        g = jnp.where(is_decay, neg_a * jax.nn.softplus(zs + dt_ref[...]), 0.0)
        cum_fw = jnp.dot(lower, g, preferred_element_type=F32, precision=lax.Precision.HIGHEST)
        cum_bw = jnp.dot(upper, g, preferred_element_type=F32, precision=lax.Precision.HIGHEST)
        tot = jnp.broadcast_to(jnp.sum(g, axis=0, keepdims=True), (c, LANE))
        tot = pltpu.roll(tot, G_TOT - G_CUM, 1)
        out = jnp.where(lane < G_CUM, beta,
                        jnp.where(lane < G_CUM + GDN_HEADS, cum_fw,
                                  jnp.where(lane < G_TOT, cum_bw,
                                            jnp.where(lane < G_TOT + 2 * GDN_HEADS, tot, 0.0))))
        g_ref[pl.ds(n * c, c), :] = out
        gt_ref[n] = out.T


def _gate_prep(zs, a_row, dt_row, *, rows):
    t = zs.shape[0]
    rows = min(rows, t)
    chunks = rows // GDN_CHUNK
    return pl.pallas_call(
        functools.partial(_gate_prep_body, chunks=chunks),
        grid=(t // rows,),
        in_specs=[
            pl.BlockSpec((rows, LANE), lambda i: (i, 0)),
            pl.BlockSpec((1, LANE), lambda i: (0, 0)),
            pl.BlockSpec((1, LANE), lambda i: (0, 0)),
        ],
        out_specs=[
            pl.BlockSpec((rows, LANE), lambda i: (i, 0)),
            pl.BlockSpec((chunks, LANE, GDN_CHUNK), lambda i: (i, 0, 0)),
        ],
        out_shape=[
            jax.ShapeDtypeStruct((t, LANE), F32),
            jax.ShapeDtypeStruct((t // GDN_CHUNK, LANE, GDN_CHUNK), F32),
        ],
        compiler_params=_params("parallel"),
        name="gate_prep",
    )(zs, a_row, dt_row)


def _sgu_body(u_ref, v_ref, gate_ref, lg_ref, lb_ref, ws_ref, bs_ref, y_ref, *, chunks):
    c = SGU_CHUNK
    v = jax.nn.gelu(v_ref[0])
    mu = jnp.mean(v, axis=-1, keepdims=True)
    vc = v - mu
    var = jnp.mean(vc * vc, axis=-1, keepdims=True)
    vn = (vc * lax.rsqrt(var + EPS) * lg_ref[...] + lb_ref[...]).astype(BF16)
    for n in range(chunks):
        for g in range(SGU_GROUPS):
            rows, cols = slice(n * c, (n + 1) * c), slice(g * c, (g + 1) * c)
            mixed = jnp.dot(ws_ref[g], vn[rows, cols], preferred_element_type=F32) + bs_ref[g]
            gate = gate_ref[0, rows, cols]
            y = jax.nn.gelu(u_ref[0, rows, cols]) * mixed * (gate * jax.nn.sigmoid(gate))
            y_ref[0, rows, cols] = y.astype(y_ref.dtype)


def _sgu(z3, ln_g, ln_b, w_s, b_s, *, rows):
    b, s, _ = z3.shape
    rows = min(rows, s)
    chunks = rows // SGU_CHUNK
    col = lambda k: pl.BlockSpec((1, rows, W_A), lambda i, r, k=k: (i, r, k))
    return pl.pallas_call(
        functools.partial(_sgu_body, chunks=chunks),
        grid=(b, s // rows),
        in_specs=[
            col(0), col(1), col(2),
            pl.BlockSpec((1, W_A), lambda i, r: (0, 0)),
            pl.BlockSpec((1, W_A), lambda i, r: (0, 0)),
            pl.BlockSpec((SGU_GROUPS, SGU_CHUNK, SGU_CHUNK), lambda i, r: (0, 0, 0)),
            pl.BlockSpec((SGU_GROUPS, SGU_CHUNK, LANE), lambda i, r: (0, 0, 0)),
        ],
        out_specs=pl.BlockSpec((1, rows, W_A), lambda i, r: (i, r, 0)),
        out_shape=jax.ShapeDtypeStruct((b, s, W_A), BF16),
        compiler_params=_params("parallel", "parallel"),
        name="sgu",
    )(z3, z3, z3, ln_g.reshape(1, W_A), ln_b.reshape(1, W_A), w_s.astype(BF16),
      jnp.broadcast_to(b_s[:, :, None], (SGU_GROUPS, SGU_CHUNK, LANE)))


XA_PAIR = 2 * XA_HEAD_DIM


def _xattn_body(q_ref, gate_ref, k_ref, v_ref, y_ref):
    for hh in range(XA_PAIR // XA_HEAD_DIM):
        cols = slice(hh * XA_HEAD_DIM, (hh + 1) * XA_HEAD_DIM)
        q = q_ref[0, :, cols].astype(BF16)
        s = lax.dot_general(q, k_ref[0, :, cols], (((1,), (1,)), ((), ())), preferred_element_type=F32)
        s = s * (XA_HEAD_DIM ** -0.5)
        s = s - jnp.max(s, axis=-1, keepdims=True)
        p = jnp.exp(s)
        p = p / jnp.sum(p, axis=-1, keepdims=True)
        o = jnp.dot(p.astype(BF16), v_ref[0, :, cols], preferred_element_type=F32)
        gate = gate_ref[0, :, cols]
        y_ref[0, :, cols] = (o * (gate * jax.nn.sigmoid(gate))).astype(y_ref.dtype)


def _xattn(z3, kv3, *, rows):
    b, s, _ = z3.shape
    rows = min(rows, s)
    n_mem = kv3.shape[1]
    pairs = W_C // XA_PAIR
    q_blk = (3 * W_A + 4 * W_B) // XA_PAIR
    gate_blk = q_blk + pairs
    return pl.pallas_call(
        _xattn_body,
        grid=(b, s // rows, pairs),
        in_specs=[
            pl.BlockSpec((1, rows, XA_PAIR), lambda i, r, h: (i, r, q_blk + h)),
            pl.BlockSpec((1, rows, XA_PAIR), lambda i, r, h: (i, r, gate_blk + h)),
            pl.BlockSpec((1, n_mem, XA_PAIR), lambda i, r, h: (i, 0, h)),
            pl.BlockSpec((1, n_mem, XA_PAIR), lambda i, r, h: (i, 0, pairs + h)),
        ],
        out_specs=pl.BlockSpec((1, rows, XA_PAIR), lambda i, r, h: (i, r, h)),
        out_shape=jax.ShapeDtypeStruct((b, s, W_C), BF16),
        compiler_params=_params("parallel", "parallel", "parallel"),
        name="xattn",
    )(z3, z3, kv3, kv3)


def _gdn_masks():
    c = GDN_CHUNK
    ri = np.arange(c)[:, None]
    ci = np.arange(c)[None, :]
    x = ri ^ ci
    halves = []
    for strict, incl in ((ri > ci, ri >= ci), (ri < ci, ri <= ci)):
        halves.append(np.stack([
            np.where(incl, 0.0, MASKED_OUT),
            ri == ci,
            -1.0 * (strict & (x < 16)),
            strict & ((x >> 4) == 1),
            strict & ((x >> 5) == 1),
            strict & ((x >> 6) == 1),
        ]).astype(np.float32))
    masks = np.concatenate(halves, axis=2)
    return jnp.asarray(masks), jnp.asarray(masks[M_NEG_DIAG16:], dtype=BF16)


def _gdn_body(q_ref, k_ref, v_ref, gate_ref, g_ref, gt_ref, cq_ref, ck_ref, cv_ref, ng_ref, m_ref,
              mb_ref, y_ref,
              qn_s, kn_s, vn_s, qk0_s, kk_s,
              p_s, bd_s, off_s, qkd_s, rhs_s, kg_s, qg_s, uw_s,
              xq_s, c_s, op_s, eg_s, sbd_s, st_s, o_s, *, seq, n_items):
    c = GDN_CHUNK
    n_chunks = seq // c
    step = pl.program_id(0)
    h = jnp.minimum(step, n_items - 1) % GDN_HEADS
    lane = lax.broadcasted_iota(jnp.int32, (c, LANE), 1)
    lo, hi = slice(0, c), slice(c, 2 * c)
    halves = (lo, hi)

    @pl.when(step == 0)
    def _():
        for ref in (p_s, bd_s, off_s, qkd_s, rhs_s, kg_s, qg_s, uw_s, xq_s, c_s, op_s, eg_s, sbd_s,
                    st_s, o_s):
            ref[...] = jnp.zeros(ref.shape, ref.dtype)

    def after(token):
        bits = pltpu.bitcast(token, jnp.uint32)
        bits = lax.shift_right_logical(lax.shift_right_logical(bits, jnp.uint32(16)), jnp.uint32(16))
        return pltpu.bitcast(bits, F32)

    def conv_silu(x_ref, cw_ref, ci_, zero):
        halo = SUBLANE
        c0 = ci_ * c
        left = (CONV_K - 1) // 2
        cw = cw_ref[...]
        if zero is not None:
            cw = cw + zero[:CONV_K]
        if 0 < ci_ < n_chunks - 1:
            taps = [x_ref[0, c0 + j - left:c0 + j - left + c, :] for j in range(CONV_K)]
        else:
            zeros = jnp.zeros((halo, HEAD_DIM), F32)
            prev = x_ref[0, c0 - halo:c0, :] if ci_ > 0 else zeros
            nxt = x_ref[0, c0 + c:c0 + c + halo, :] if ci_ < n_chunks - 1 else zeros
            win = jnp.concatenate([prev, x_ref[0, c0:c0 + c, :], nxt], axis=0)
            taps = [win[halo + j - left:halo + j - left + c, :] for j in range(CONV_K)]
        acc = None
        for j in range(CONV_K):
            term = taps[j] * cw[j:j + 1, :]
            acc = term if acc is None else acc + term
        return acc * jax.nn.sigmoid(acc)

    def l2n(x, scale=1.0):
        return x * (lax.rsqrt(jnp.sum(x * x, axis=-1, keepdims=True) + EPS) * scale)

    def column(gc, idx):
        return jnp.sum(jnp.where(lane == idx, gc, 0.0), axis=1, keepdims=True)

    def gt_row(ci_, idx):
        return gt_ref[0, ci_, pl.ds(idx, 1), :]

    def set_bd(buf, j, x):
        buf[j, lo, lo] = x[:, lo]
        buf[j, hi, hi] = x[:, hi]

    def packed_lhs(buf, j):
        return jnp.concatenate([buf[j, lo, lo], buf[j, hi, hi]], axis=1)

    cached = set()

    def chunk_inputs(ci_, zero):
        if ci_ in cached:
            return qn_s[ci_], kn_s[ci_], vn_s[ci_], qk0_s[ci_], kk_s[ci_]
        cached.add(ci_)
        q = l2n(conv_silu(q_ref, cq_ref, ci_, zero), HEAD_DIM ** -0.5)
        k = l2n(conv_silu(k_ref, ck_ref, ci_, zero))
        v = conv_silu(v_ref, cv_ref, ci_, zero)
        kb = k.astype(BF16)
        qkk = lax.dot_general(jnp.concatenate([q.astype(BF16), kb], axis=0), kb,
                              (((1,), (1,)), ((), ())), preferred_element_type=F32)
        qn_s[ci_], kn_s[ci_], vn_s[ci_] = q, k, v
        qk0_s[ci_], kk_s[ci_] = qkk[:c], qkk[c:]
        return q, k, v, qkk[:c], qkk[c:]

    def build(s, token=None):
        zero = None if token is None else after(token)
        kd, qkd = [], []
        for d in range(2):
            ci_ = s if d == 0 else n_chunks - 1 - s
            q, k, v, qk0, kk = chunk_inputs(ci_, zero)
            gc = g_ref[0, ci_ * c:(ci_ + 1) * c, :]
            cum_r = gt_row(ci_, G_CUM + GDN_HEADS * d + h)
            tot_r = gt_row(ci_, G_TOT + GDN_HEADS * d + h)
            if zero is not None:
                gc = gc + jnp.concatenate([zero] * (c // SUBLANE), axis=0)
                cum_r, tot_r = cum_r + zero[:1], tot_r + zero[:1]
            beta = column(gc, G_BETA + GDN_HEADS * d + h)
            cum_c = column(gc, G_CUM + GDN_HEADS * d + h)
            e = jnp.exp(jnp.minimum(cum_c - cum_r, m_ref[M_BOUND, :, halves[d]]))
            e_c = jnp.exp(cum_c)
            kd.append(kk * beta * e)
            qkd.append(qk0 * e)
            rhs_s[s, d] = jnp.concatenate([v * beta, k * (beta * e_c)], axis=1).astype(BF16)
            kg_s[s, d] = (k * jnp.exp(tot_r - cum_c)).astype(BF16)
            qg_s[s, :, halves[d]] = q * e_c
            eg_s[s, :, halves[d]] = jnp.broadcast_to(jnp.exp(tot_r), (SUBLANE, c))
        kd = jnp.concatenate(kd, axis=1)
        qkd_s[s] = jnp.concatenate(qkd, axis=1).astype(BF16)
        p_s[s] = m_ref[M_EYE] + kd * m_ref[M_NEG_DIAG16]
        kd = kd.astype(BF16)
        set_bd(bd_s, s, kd * mb_ref[0])
        for lvl in range(3):
            off_s[s, lvl] = kd * mb_ref[1 + lvl]

    def neumann_first(j):
        n2 = jnp.dot(packed_lhs(bd_s, j), bd_s[j], preferred_element_type=F32)
        set_bd(bd_s, j, n2.astype(BF16))

    def neumann_step(j, last):
        n = packed_lhs(bd_s, j)
        rhs = bd_s[j]
        p = p_s[j]
        p_s[j] = p + jnp.dot(p.astype(BF16), rhs, preferred_element_type=F32)
        if not last:
            set_bd(bd_s, j, jnp.dot(n, rhs, preferred_element_type=F32).astype(BF16))

    def merge_a(j, lvl):
        set_bd(bd_s, j, p_s[j].astype(BF16))
        x = jnp.dot(off_s[j, lvl], bd_s[j], preferred_element_type=F32)
        set_bd(bd_s, j, x.astype(BF16))

    def merge_b(j):
        t = p_s[j]
        p_s[j] = t - jnp.dot(t.astype(BF16), bd_s[j], preferred_element_type=F32)

    def apply_t(j):
        t = p_s[j].astype(BF16)
        for d in range(2):
            uw = jnp.dot(t[:, halves[d]], rhs_s[j, d], preferred_element_type=F32)
            uw_s[j, d] = uw.astype(BF16)

    def finish(j):
        for d in range(2):
            uw = uw_s[j, d]
            cx = lax.dot_general(kg_s[j, d], uw, (((0,), (0,)), ((), ())), preferred_element_type=F32)
            ow = jnp.dot(qkd_s[j, :, halves[d]], uw, preferred_element_type=F32)
            c_s[j, :, halves[d]] = cx[:, lo]
            op_s[j, :, halves[d]] = ow[:, lo]
            xq_s[j, lo, halves[d]] = cx[:, hi].astype(BF16)
            xq_s[j, hi, halves[d]] = (qg_s[j, :, halves[d]] - ow[:, hi]).astype(BF16)
        return cx[:SUBLANE, lo]

    stages = [
        build,
        neumann_first,
        functools.partial(neumann_step, last=False),
        functools.partial(neumann_step, last=False),
        functools.partial(neumann_step, last=True),
        functools.partial(merge_a, lvl=0), merge_b,
        functools.partial(merge_a, lvl=1), merge_b,
        functools.partial(merge_a, lvl=2), merge_b,
        apply_t,
        finish,
    ]

    def scan_step(s):
        buf = s % 2
        state = st_s[...]
        set_bd(sbd_s, buf, state.astype(BF16))
        rhs = sbd_s[buf]
        xs = jnp.dot(xq_s[s, lo, :], rhs, preferred_element_type=F32)
        out = jnp.dot(xq_s[s, hi, :], rhs, preferred_element_type=F32) + op_s[s]
        o_s[0, s * c:(s + 1) * c, :] = out[:, lo]
        o_s[1, (n_chunks - 1 - s) * c:(n_chunks - s) * c, :] = out[:, hi]
        st_s[...] = state * eg_s[s, 0:1, :] - xs + c_s[s]

    def write_output():
        o = o_s[0] + o_s[1]
        o = o * lax.rsqrt(jnp.mean(o * o, axis=-1, keepdims=True) + EPS) * ng_ref[...]
        gate = gate_ref[0]
        y_ref[0] = (o * (gate * jax.nn.sigmoid(gate))).astype(y_ref.dtype)

    last = len(stages) - 1
    assert n_chunks >= last
    tokens = {}

    def run_stage(stage, s, t):
        if stage == 0:
            build(s, tokens.get(t - BUILD_LAG))
        elif stage == last:
            tokens[t] = finish(s)
        else:
            stages[stage](s)

    for t in range(n_chunks):
        for s in range(n_chunks):
            if 1 <= n_chunks + t - s <= last:
                run_stage(n_chunks + t - s, s, t)
        s_done = n_chunks + t - last
        if s_done < n_chunks:
            scan_step(s_done)
            if s_done == n_chunks - 1:
                write_output()
        for s in range(n_chunks):
            if 0 <= t - s <= last:
                run_stage(t - s, s, t)
        if t >= last:
            if t == last:
                st_s[...] = jnp.zeros(st_s.shape, F32)
            scan_step(t - last)


def _gdn(z3, g3, gt4, conv_w, norm_g):
    b, s, _ = z3.shape
    c = GDN_CHUNK
    n_chunks = s // c
    n_items = b * GDN_HEADS
    blk0 = 3 * W_A // HEAD_DIM
    started = lambda i: jnp.minimum(i, n_items - 1)
    finished = lambda i: jnp.maximum(i - 1, 0)
    head = lambda k: pl.BlockSpec(
        (1, s, HEAD_DIM),
        lambda i, k=k: (started(i) // GDN_HEADS, 0, blk0 + k * GDN_HEADS + started(i) % GDN_HEADS))
    cw = lambda k: pl.BlockSpec(
        (CONV_K, HEAD_DIM), lambda i, k=k: (0, k * GDN_HEADS + started(i) % GDN_HEADS))
    return pl.pallas_call(
        functools.partial(_gdn_body, seq=s, n_items=n_items),
        grid=(n_items + 1,),
        in_specs=[
            head(0), head(1), head(2),
            pl.BlockSpec((1, s, HEAD_DIM),
                         lambda i: (finished(i) // GDN_HEADS, 0,
                                    blk0 + 3 * GDN_HEADS + finished(i) % GDN_HEADS)),
            pl.BlockSpec((1, s, LANE), lambda i: (started(i) // GDN_HEADS, 0, 0)),
            pl.BlockSpec((1, n_chunks, LANE, c), lambda i: (started(i) // GDN_HEADS, 0, 0, 0)),
            cw(0), cw(1), cw(2),
            pl.BlockSpec((1, HEAD_DIM), lambda i: (0, 0)),
            pl.BlockSpec((6, c, 2 * c), lambda i: (0, 0, 0)),
            pl.BlockSpec((4, c, 2 * c), lambda i: (0, 0, 0)),
        ],
        out_specs=pl.BlockSpec((1, s, HEAD_DIM),
                               lambda i: (finished(i) // GDN_HEADS, 0, finished(i) % GDN_HEADS)),
        out_shape=jax.ShapeDtypeStruct((b, s, W_B), BF16),
        scratch_shapes=[
            pltpu.VMEM((n_chunks, c, HEAD_DIM), F32),
            pltpu.VMEM((n_chunks, c, HEAD_DIM), F32),
            pltpu.VMEM((n_chunks, c, HEAD_DIM), F32),
            pltpu.VMEM((n_chunks, c, c), F32),
            pltpu.VMEM((n_chunks, c, c), F32),
            pltpu.VMEM((n_chunks, c, 2 * c), F32),
            pltpu.VMEM((n_chunks, 2 * c, 2 * c), BF16),
            pltpu.VMEM((n_chunks, 3, c, 2 * c), BF16),
            pltpu.VMEM((n_chunks, c, 2 * c), BF16),
            pltpu.VMEM((n_chunks, 2, c, 2 * c), BF16),
            pltpu.VMEM((n_chunks, 2, c, HEAD_DIM), BF16),
            pltpu.VMEM((n_chunks, c, 2 * c), F32),
            pltpu.VMEM((n_chunks, 2, c, 2 * c), BF16),
            pltpu.VMEM((n_chunks, 2 * c, 2 * c), BF16),
            pltpu.VMEM((n_chunks, c, 2 * c), F32),
            pltpu.VMEM((n_chunks, c, 2 * c), F32),
            pltpu.VMEM((n_chunks, SUBLANE, 2 * c), F32),
            pltpu.VMEM((2, 2 * c, 2 * c), BF16),
            pltpu.VMEM((c, 2 * c), F32),
            pltpu.VMEM((2, s, HEAD_DIM), F32),
        ],
        compiler_params=_params("arbitrary"),
        name="gdn",
    )(z3, z3, z3, z3, g3, gt4, conv_w, conv_w, conv_w, norm_g.reshape(1, HEAD_DIM), *_gdn_masks())


def _out_proj_body(ya_ref, yb_ref, yc_ref, wa_ref, wb_ref, wc_ref, x_ref, *rest, final_norm):
    acc = jnp.dot(ya_ref[...], wa_ref[...], preferred_element_type=F32)
    acc += jnp.dot(yb_ref[...], wb_ref[...], preferred_element_type=F32)
    acc += jnp.dot(yc_ref[...], wc_ref[...], preferred_element_type=F32)
    if not final_norm:
        (o_ref,) = rest
        o_ref[...] = x_ref[...] + acc
        return
    g_ref, o_ref, row_s = rest
    j = pl.program_id(1)
    n_col, _, tn = row_s.shape
    row_s[j] = x_ref[...] + acc

    @pl.when(j == n_col - 1)
    def _():
        ss = None
        for jj in range(n_col):
            part = jnp.sum(row_s[jj] * row_s[jj], axis=-1, keepdims=True)
            ss = part if ss is None else ss + part
        scale = lax.rsqrt(ss / (n_col * tn) + EPS)
        for jj in range(n_col):
            cols = slice(jj * tn, (jj + 1) * tn)
            o_ref[:, cols] = row_s[jj] * scale * g_ref[:, cols]


def _out_proj(ya, yb, yc, w_out, x, final_g=None, *, tm, tn):
    t, d = x.shape
    tm = min(tm, t)
    final_norm = final_g is not None
    row = lambda w: pl.BlockSpec((tm, w), lambda i, j: (i, 0))
    wrows = lambda w, first: pl.BlockSpec((w, tn), lambda i, j: (first // w, j))
    in_specs = [row(W_A), row(W_B), row(W_C),
                wrows(W_A, 0), wrows(W_B, W_A), wrows(W_C, W_A + W_B),
                pl.BlockSpec((tm, tn), lambda i, j: (i, j))]
    assert W_A % W_B == 0 and (W_A + W_B) % W_C == 0
    args = [ya, yb, yc, w_out, w_out, w_out, x]
    if final_norm:
        in_specs.append(pl.BlockSpec((1, d), lambda i, j: (0, 0)))
        args.append(final_g.reshape(1, d))
        out_spec = pl.BlockSpec((tm, d), lambda i, j: (i, 0))
        scratch = [pltpu.VMEM((d // tn, tm, tn), F32)]
    else:
        out_spec = pl.BlockSpec((tm, tn), lambda i, j: (i, j))
        scratch = []
    return pl.pallas_call(
        functools.partial(_out_proj_body, final_norm=final_norm),
        grid=(t // tm, d // tn),
        in_specs=in_specs,
        out_specs=out_spec,
        out_shape=jax.ShapeDtypeStruct((t, d), F32),
        scratch_shapes=scratch,
        compiler_params=_params("parallel", "arbitrary" if final_norm else "parallel"),
        name="out_proj_norm" if final_norm else "out_proj",
    )(*args)


def _lane_row(values, offset):
    flat = values.reshape(-1).astype(F32)
    return jnp.zeros((1, LANE), F32).at[0, offset:offset + flat.shape[0]].set(flat)


def _prep_w_in_body(w_ref, o_ref):
    w = w_ref[0]
    o_ref[:, :OFF_SMALL] = w[:, :OFF_SMALL]
    o_ref[:, OFF_SMALL:W_MAIN] = w[:, OFF_SMALL + N_SMALL:]
    small = w[:, OFF_SMALL:OFF_SMALL + N_SMALL]
    pad = jnp.zeros((small.shape[0], LANE - N_SMALL), small.dtype)
    o_ref[:, W_MAIN:] = jnp.concatenate([small, pad], axis=1)


def _prep_w_in(w_in, layer, *, rows):
    _, d, n_in = w_in.shape
    return pl.pallas_call(
        _prep_w_in_body,
        grid=(d // rows,),
        in_specs=[pl.BlockSpec((1, rows, n_in), lambda i: (layer, i, 0))],
        out_specs=pl.BlockSpec((rows, W_MAIN + LANE), lambda i: (i, 0)),
        out_shape=jax.ShapeDtypeStruct((d, W_MAIN + LANE), w_in.dtype),
        compiler_params=_params("parallel"),
        name="prep_w_in",
    )(w_in)


def _layer_weights(w_in_bf16, w_mem_kv, w_out, layer):
    return (_prep_w_in(w_in_bf16, layer, rows=512), w_mem_kv[layer].astype(BF16),
            w_out[layer].astype(BF16))


def _layer(x, mem, weights, norm_g, sgu_ln_g, sgu_ln_b, sgu_w, sgu_b, conv_w, a_log, dt_bias,
           gdn_norm_g, mem_norm_g, final_g):
    b, s, d = x.shape
    t = b * s
    x2 = x.reshape(t, d)
    w_all, w_kv, w_out = weights
    z, zs = _norm_matmul(x2, norm_g, w_all, W_MAIN // LANE, tm=512, tn=1280, out_dtype=F32)
    z3 = z.reshape(b, s, W_MAIN)

    g, gt = _gate_prep(zs, _lane_row(a_log, G_CUM), _lane_row(dt_bias, G_CUM), rows=512)
    g3 = g.reshape(b, s, LANE)
    gt4 = gt.reshape(b, s // GDN_CHUNK, LANE, GDN_CHUNK)

    ya = _sgu(z3, sgu_ln_g, sgu_ln_b, sgu_w, sgu_b, rows=512)
    yb = _gdn(z3, g3, gt4, conv_w, gdn_norm_g)
    kv = _norm_matmul(mem.reshape(-1, d), mem_norm_g, w_kv, tm=512, tn=1024, out_dtype=BF16)
    yc = _xattn(z3, kv.reshape(b, -1, 2 * W_C), rows=1024)

    if final_g is None:
        out = _out_proj(ya.reshape(t, W_A), yb.reshape(t, W_B), yc.reshape(t, W_C), w_out, x2,
                        tm=1024, tn=512)
    else:
        out = _out_proj(ya.reshape(t, W_A), yb.reshape(t, W_B), yc.reshape(t, W_C), w_out, x2,
                        final_g, tm=512, tn=512)
    return out.reshape(b, s, d)


def kernel(x_prompt, x_sample, mem_prompt, mem_sample, norm_g, w_in, sgu_ln_g, sgu_ln_b, sgu_w,
           sgu_b, conv_w, a_log, dt_bias, gdn_norm_g, mem_norm_g, w_mem_kv, w_out, final_g):
    depth = norm_g.shape[0]
    w_in_bf16 = w_in.astype(BF16)
    weights = [_layer_weights(w_in_bf16, w_mem_kv, w_out, l) for l in range(depth)]

    def trunk(x, mem):
        for l in range(depth):
            x = _layer(x, mem, weights[l], norm_g[l], sgu_ln_g[l], sgu_ln_b[l], sgu_w[l], sgu_b[l],
                       conv_w[l], a_log[l], dt_bias[l], gdn_norm_g[l], mem_norm_g[l],
                       final_g if l == depth - 1 else None)
        return x

    return trunk(x_prompt, mem_prompt), trunk(x_sample, mem_sample)
```

```python
import functools

import jax
import jax.numpy as jnp
import numpy as np
from jax import lax
from jax.experimental import pallas as pl
from jax.experimental.pallas import tpu as pltpu

F32 = jnp.float32
BF16 = jnp.bfloat16
EPS = 1e-6

SGU_CHUNK = 128
SGU_GROUPS = 12
GDN_HEADS = 12
HEAD_DIM = 128
CONV_K = 5
XA_HEADS = 4
XA_HEAD_DIM = 256
W_A = SGU_GROUPS * SGU_CHUNK
W_B = GDN_HEADS * HEAD_DIM
W_C = XA_HEADS * XA_HEAD_DIM
W_MAIN = 3 * W_A + 4 * W_B + 2 * W_C
N_SMALL = 4 * GDN_HEADS
OFF_SMALL = 3 * W_A + 4 * W_B

LANE = 128
SUBLANE = 8
VMEM_LIMIT = 56 * 1024 * 1024

GDN_CHUNK = 128
G_BETA, G_CUM, G_TOT = 0, 2 * GDN_HEADS, 4 * GDN_HEADS
M_BOUND, M_EYE, M_NEG_DIAG16, M_OFF16, M_OFF32, M_OFF64 = range(6)
MASKED_OUT = -1e30


def _params(*sem):
    return pltpu.CompilerParams(dimension_semantics=sem, vmem_limit_bytes=VMEM_LIMIT)


def _norm_matmul_body(*refs, has_small):
    if has_small:
        x_ref, g_ref, w_ref, ws_ref, z_ref, zs_ref, h_ref = refs
    else:
        x_ref, g_ref, w_ref, z_ref, h_ref = refs

    @pl.when(pl.program_id(1) == 0)
    def _():
        x = x_ref[...]
        ms = jnp.mean(x * x, axis=-1, keepdims=True)
        h = (x * lax.rsqrt(ms + EPS) * g_ref[...]).astype(BF16)
        h_ref[...] = h
        if has_small:
            zs_ref[...] = jnp.dot(h, ws_ref[...], preferred_element_type=F32)

    z_ref[...] = jnp.dot(h_ref[...], w_ref[...], preferred_element_type=F32).astype(z_ref.dtype)


def _norm_matmul(x, g, w, small_block=None, *, tm, tn, out_dtype):
    t, d = x.shape
    has_small = small_block is not None
    n = small_block * LANE if has_small else w.shape[1]
    tm = min(tm, t)
    in_specs = [
        pl.BlockSpec((tm, d), lambda i, j: (i, 0)),
        pl.BlockSpec((1, d), lambda i, j: (0, 0)),
        pl.BlockSpec((d, tn), lambda i, j: (0, j)),
    ]
    out_shape = [jax.ShapeDtypeStruct((t, n), out_dtype)]
    out_specs = [pl.BlockSpec((tm, tn), lambda i, j: (i, j))]
    args = [x, g.reshape(1, d), w]
    if has_small:
        in_specs.append(pl.BlockSpec((d, LANE), lambda i, j: (0, small_block)))
        out_shape.append(jax.ShapeDtypeStruct((t, LANE), F32))
        out_specs.append(pl.BlockSpec((tm, LANE), lambda i, j: (i, 0)))
        args.append(w)
    outs = pl.pallas_call(
        functools.partial(_norm_matmul_body, has_small=has_small),
        grid=(t // tm, n // tn),
        in_specs=in_specs,
        out_specs=out_specs,
        out_shape=out_shape,
        scratch_shapes=[pltpu.VMEM((tm, d), BF16)],
        compiler_params=_params("parallel", "arbitrary"),
        name="norm_matmul_small" if has_small else "norm_matmul",
    )(*args)
    return outs if has_small else outs[0]


def _gate_prep_body(zs_ref, a_ref, dt_ref, g_ref, gt_ref, *, chunks):
    c = GDN_CHUNK
    lane = lax.broadcasted_iota(jnp.int32, (c, LANE), 1)
    ri = lax.broadcasted_iota(jnp.int32, (c, c), 0)
    ci = lax.broadcasted_iota(jnp.int32, (c, c), 1)
    lower = (ri >= ci).astype(F32)
    upper = (ri <= ci).astype(F32)
    is_decay = (lane >= G_CUM) & (lane < G_TOT)
    neg_a = -jnp.exp(a_ref[...])
    for n in range(chunks):
        zs = zs_ref[pl.ds(n * c, c), :]
        beta = jax.nn.sigmoid(zs)
        g = jnp.where(is_decay, neg_a * jax.nn.softplus(zs + dt_ref[...]), 0.0)
        cum_fw = jnp.dot(lower, g, preferred_element_type=F32, precision=lax.Precision.HIGHEST)
        cum_bw = jnp.dot(upper, g, preferred_element_type=F32, precision=lax.Precision.HIGHEST)
        tot = jnp.broadcast_to(jnp.sum(g, axis=0, keepdims=True), (c, LANE))
        tot = pltpu.roll(tot, G_TOT - G_CUM, 1)
        out = jnp.where(lane < G_CUM, beta,
                        jnp.where(lane < G_CUM + GDN_HEADS, cum_fw,
                                  jnp.where(lane < G_TOT, cum_bw,
                                            jnp.where(lane < G_TOT + 2 * GDN_HEADS, tot, 0.0))))
        g_ref[pl.ds(n * c, c), :] = out
        gt_ref[n] = out.T


def _gate_prep(zs, a_row, dt_row, *, rows):
    t = zs.shape[0]
    rows = min(rows, t)
    chunks = rows // GDN_CHUNK
    return pl.pallas_call(
        functools.partial(_gate_prep_body, chunks=chunks),
        grid=(t // rows,),
        in_specs=[
            pl.BlockSpec((rows, LANE), lambda i: (i, 0)),
            pl.BlockSpec((1, LANE), lambda i: (0, 0)),
            pl.BlockSpec((1, LANE), lambda i: (0, 0)),
        ],
        out_specs=[
            pl.BlockSpec((rows, LANE), lambda i: (i, 0)),
            pl.BlockSpec((chunks, LANE, GDN_CHUNK), lambda i: (i, 0, 0)),
        ],
        out_shape=[
            jax.ShapeDtypeStruct((t, LANE), F32),
            jax.ShapeDtypeStruct((t // GDN_CHUNK, LANE, GDN_CHUNK), F32),
        ],
        compiler_params=_params("parallel"),
        name="gate_prep",
    )(zs, a_row, dt_row)


def _sgu_body(u_ref, v_ref, gate_ref, lg_ref, lb_ref, ws_ref, bs_ref, y_ref, *, chunks):
    c = SGU_CHUNK
    v = jax.nn.gelu(v_ref[0])
    mu = jnp.mean(v, axis=-1, keepdims=True)
    vc = v - mu
    var = jnp.mean(vc * vc, axis=-1, keepdims=True)
    vn = (vc * lax.rsqrt(var + EPS) * lg_ref[...] + lb_ref[...]).astype(BF16)
    for n in range(chunks):
        for g in range(SGU_GROUPS):
            rows, cols = slice(n * c, (n + 1) * c), slice(g * c, (g + 1) * c)
            mixed = jnp.dot(ws_ref[g], vn[rows, cols], preferred_element_type=F32) + bs_ref[g]
            gate = gate_ref[0, rows, cols]
            y = jax.nn.gelu(u_ref[0, rows, cols]) * mixed * (gate * jax.nn.sigmoid(gate))
            y_ref[0, rows, cols] = y.astype(y_ref.dtype)


def _sgu(z3, ln_g, ln_b, w_s, b_s, *, rows):
    b, s, _ = z3.shape
    rows = min(rows, s)
    chunks = rows // SGU_CHUNK
    col = lambda k: pl.BlockSpec((1, rows, W_A), lambda i, r, k=k: (i, r, k))
    return pl.pallas_call(
        functools.partial(_sgu_body, chunks=chunks),
        grid=(b, s // rows),
        in_specs=[
            col(0), col(1), col(2),
            pl.BlockSpec((1, W_A), lambda i, r: (0, 0)),
            pl.BlockSpec((1, W_A), lambda i, r: (0, 0)),
            pl.BlockSpec((SGU_GROUPS, SGU_CHUNK, SGU_CHUNK), lambda i, r: (0, 0, 0)),
            pl.BlockSpec((SGU_GROUPS, SGU_CHUNK, LANE), lambda i, r: (0, 0, 0)),
        ],
        out_specs=pl.BlockSpec((1, rows, W_A), lambda i, r: (i, r, 0)),
        out_shape=jax.ShapeDtypeStruct((b, s, W_A), BF16),
        compiler_params=_params("parallel", "parallel"),
        name="sgu",
    )(z3, z3, z3, ln_g.reshape(1, W_A), ln_b.reshape(1, W_A), w_s.astype(BF16),
      jnp.broadcast_to(b_s[:, :, None], (SGU_GROUPS, SGU_CHUNK, LANE)))


XA_PAIR = 2 * XA_HEAD_DIM


def _xattn_body(q_ref, gate_ref, k_ref, v_ref, y_ref):
    for hh in range(XA_PAIR // XA_HEAD_DIM):
        cols = slice(hh * XA_HEAD_DIM, (hh + 1) * XA_HEAD_DIM)
        q = q_ref[0, :, cols].astype(BF16)
        s = lax.dot_general(q, k_ref[0, :, cols], (((1,), (1,)), ((), ())), preferred_element_type=F32)
        s = s * (XA_HEAD_DIM ** -0.5)
        s = s - jnp.max(s, axis=-1, keepdims=True)
        p = jnp.exp(s)
        p = p / jnp.sum(p, axis=-1, keepdims=True)
        o = jnp.dot(p.astype(BF16), v_ref[0, :, cols], preferred_element_type=F32)
        gate = gate_ref[0, :, cols]
        y_ref[0, :, cols] = (o * (gate * jax.nn.sigmoid(gate))).astype(y_ref.dtype)


def _xattn(z3, kv3, *, rows):
    b, s, _ = z3.shape
    rows = min(rows, s)
    n_mem = kv3.shape[1]
    pairs = W_C // XA_PAIR
    q_blk = (3 * W_A + 4 * W_B) // XA_PAIR
    gate_blk = q_blk + pairs
    return pl.pallas_call(
        _xattn_body,
        grid=(b, s // rows, pairs),
        in_specs=[
            pl.BlockSpec((1, rows, XA_PAIR), lambda i, r, h: (i, r, q_blk + h)),
            pl.BlockSpec((1, rows, XA_PAIR), lambda i, r, h: (i, r, gate_blk + h)),
            pl.BlockSpec((1, n_mem, XA_PAIR), lambda i, r, h: (i, 0, h)),
            pl.BlockSpec((1, n_mem, XA_PAIR), lambda i, r, h: (i, 0, pairs + h)),
        ],
        out_specs=pl.BlockSpec((1, rows, XA_PAIR), lambda i, r, h: (i, r, h)),
        out_shape=jax.ShapeDtypeStruct((b, s, W_C), BF16),
        compiler_params=_params("parallel", "parallel", "parallel"),
        name="xattn",
    )(z3, z3, kv3, kv3)


def _gdn_masks():
    c = GDN_CHUNK
    ri = np.arange(c)[:, None]
    ci = np.arange(c)[None, :]
    x = ri ^ ci
    halves = []
    for strict, incl in ((ri > ci, ri >= ci), (ri < ci, ri <= ci)):
        halves.append(np.stack([
            np.where(incl, 0.0, MASKED_OUT),
            ri == ci,
            -1.0 * (strict & (x < 16)),
            strict & ((x >> 4) == 1),
            strict & ((x >> 5) == 1),
            strict & ((x >> 6) == 1),
        ]).astype(np.float32))
    masks = np.concatenate(halves, axis=2)
    return jnp.asarray(masks), jnp.asarray(masks[M_NEG_DIAG16:], dtype=BF16)


def _gdn_body(q_ref, k_ref, v_ref, gate_ref, g_ref, gt_ref, cq_ref, ck_ref, cv_ref, ng_ref, m_ref,
              mb_ref, y_ref,
              qn_s, kn_s, vn_s, qk0_s, kk_s,
              p_s, bd_s, off_s, qkd_s, rhs_s, kg_s, qg_s, uw_s,
              xq_s, c_s, op_s, eg_s, sbd_s, st_s, o_s, *, seq):
    c = GDN_CHUNK
    n_chunks = seq // c
    h = pl.program_id(1)
    lane = lax.broadcasted_iota(jnp.int32, (c, LANE), 1)
    lo, hi = slice(0, c), slice(c, 2 * c)
    halves = (lo, hi)

    bd_s[...] = jnp.zeros(bd_s.shape, BF16)
    sbd_s[...] = jnp.zeros(sbd_s.shape, BF16)
    st_s[...] = jnp.zeros(st_s.shape, F32)

    def conv_silu(x_ref, cw_ref, ci_):
        halo = SUBLANE
        c0 = ci_ * c
        left = (CONV_K - 1) // 2
        cw = cw_ref[...]
        if 0 < ci_ < n_chunks - 1:
            taps = [x_ref[0, c0 + j - left:c0 + j - left + c, :] for j in range(CONV_K)]
        else:
            zeros = jnp.zeros((halo, HEAD_DIM), F32)
            prev = x_ref[0, c0 - halo:c0, :] if ci_ > 0 else zeros
            nxt = x_ref[0, c0 + c:c0 + c + halo, :] if ci_ < n_chunks - 1 else zeros
            win = jnp.concatenate([prev, x_ref[0, c0:c0 + c, :], nxt], axis=0)
            taps = [win[halo + j - left:halo + j - left + c, :] for j in range(CONV_K)]
        acc = None
        for j in range(CONV_K):
            term = taps[j] * cw[j:j + 1, :]
            acc = term if acc is None else acc + term
        return acc * jax.nn.sigmoid(acc)

    def l2n(x, scale=1.0):
        return x * (lax.rsqrt(jnp.sum(x * x, axis=-1, keepdims=True) + EPS) * scale)

    def column(gc, idx):
        return jnp.sum(jnp.where(lane == idx, gc, 0.0), axis=1, keepdims=True)

    def gt_row(ci_, idx):
        return gt_ref[0, ci_, pl.ds(idx, 1), :]

    def set_bd(buf, j, x):
        buf[j, lo, lo] = x[:, lo]
        buf[j, hi, hi] = x[:, hi]

    def packed_lhs(buf, j):
        return jnp.concatenate([buf[j, lo, lo], buf[j, hi, hi]], axis=1)

    cached = set()

    def chunk_inputs(ci_):
        if ci_ in cached:
            return qn_s[ci_], kn_s[ci_], vn_s[ci_], qk0_s[ci_], kk_s[ci_]
        cached.add(ci_)
        q = l2n(conv_silu(q_ref, cq_ref, ci_), HEAD_DIM ** -0.5)
        k = l2n(conv_silu(k_ref, ck_ref, ci_))
        v = conv_silu(v_ref, cv_ref, ci_)
        kb = k.astype(BF16)
        qkk = lax.dot_general(jnp.concatenate([q.astype(BF16), kb], axis=0), kb,
                              (((1,), (1,)), ((), ())), preferred_element_type=F32)
        qn_s[ci_], kn_s[ci_], vn_s[ci_] = q, k, v
        qk0_s[ci_], kk_s[ci_] = qkk[:c], qkk[c:]
        return q, k, v, qkk[:c], qkk[c:]

    def build(s):
        kd, qkd = [], []
        for d in range(2):
            ci_ = s if d == 0 else n_chunks - 1 - s
            q, k, v, qk0, kk = chunk_inputs(ci_)
            gc = g_ref[0, ci_ * c:(ci_ + 1) * c, :]
            beta = column(gc, G_BETA + GDN_HEADS * d + h)
            cum_c = column(gc, G_CUM + GDN_HEADS * d + h)
            cum_r = gt_row(ci_, G_CUM + GDN_HEADS * d + h)
            tot_r = gt_row(ci_, G_TOT + GDN_HEADS * d + h)
            e = jnp.exp(jnp.minimum(cum_c - cum_r, m_ref[M_BOUND, :, halves[d]]))
            e_c = jnp.exp(cum_c)
            kd.append(kk * beta * e)
            qkd.append(qk0 * e)
            rhs_s[s, d] = jnp.concatenate([v * beta, k * (beta * e_c)], axis=1).astype(BF16)
            kg_s[s, d] = (k * jnp.exp(tot_r - cum_c)).astype(BF16)
            qg_s[s, :, halves[d]] = q * e_c
            eg_s[s, :, halves[d]] = jnp.broadcast_to(jnp.exp(tot_r), (SUBLANE, c))
        kd = jnp.concatenate(kd, axis=1)
        qkd_s[s] = jnp.concatenate(qkd, axis=1).astype(BF16)
        p_s[s] = m_ref[M_EYE] + kd * m_ref[M_NEG_DIAG16]
        kd = kd.astype(BF16)
        set_bd(bd_s, s, kd * mb_ref[0])
        for lvl in range(3):
            off_s[s, lvl] = kd * mb_ref[1 + lvl]

    def neumann_first(j):
        n2 = jnp.dot(packed_lhs(bd_s, j), bd_s[j], preferred_element_type=F32)
        set_bd(bd_s, j, n2.astype(BF16))

    def neumann_step(j, last):
        n = packed_lhs(bd_s, j)
        rhs = bd_s[j]
        p = p_s[j]
        p_s[j] = p + jnp.dot(p.astype(BF16), rhs, preferred_element_type=F32)
        if not last:
            set_bd(bd_s, j, jnp.dot(n, rhs, preferred_element_type=F32).astype(BF16))

    def merge_a(j, lvl):
        set_bd(bd_s, j, p_s[j].astype(BF16))
        x = jnp.dot(off_s[j, lvl], bd_s[j], preferred_element_type=F32)
        set_bd(bd_s, j, x.astype(BF16))

    def merge_b(j):
        t = p_s[j]
        p_s[j] = t - jnp.dot(t.astype(BF16), bd_s[j], preferred_element_type=F32)

    def apply_t(j):
        t = p_s[j].astype(BF16)
        for d in range(2):
            uw = jnp.dot(t[:, halves[d]], rhs_s[j, d], preferred_element_type=F32)
            uw_s[j, d] = uw.astype(BF16)

    def finish(j):
        for d in range(2):
            uw = uw_s[j, d]
            cx = lax.dot_general(kg_s[j, d], uw, (((0,), (0,)), ((), ())), preferred_element_type=F32)
            ow = jnp.dot(qkd_s[j, :, halves[d]], uw, preferred_element_type=F32)
            c_s[j, :, halves[d]] = cx[:, lo]
            op_s[j, :, halves[d]] = ow[:, lo]
            xq_s[j, lo, halves[d]] = cx[:, hi].astype(BF16)
            xq_s[j, hi, halves[d]] = (qg_s[j, :, halves[d]] - ow[:, hi]).astype(BF16)

    stages = [
        build,
        neumann_first,
        functools.partial(neumann_step, last=False),
        functools.partial(neumann_step, last=False),
        functools.partial(neumann_step, last=True),
        functools.partial(merge_a, lvl=0), merge_b,
        functools.partial(merge_a, lvl=1), merge_b,
        functools.partial(merge_a, lvl=2), merge_b,
        apply_t,
        finish,
    ]

    def scan_step(s):
        buf = s % 2
        state = st_s[...]
        set_bd(sbd_s, buf, state.astype(BF16))
        rhs = sbd_s[buf]
        xs = jnp.dot(xq_s[s, lo, :], rhs, preferred_element_type=F32)
        out = jnp.dot(xq_s[s, hi, :], rhs, preferred_element_type=F32) + op_s[s]
        o_s[0, s * c:(s + 1) * c, :] = out[:, lo]
        o_s[1, (n_chunks - 1 - s) * c:(n_chunks - s) * c, :] = out[:, hi]
        st_s[...] = state * eg_s[s, 0:1, :] - xs + c_s[s]

    for t in range(n_chunks + len(stages) - 1):
        for s in range(n_chunks):
            if 0 <= t - s < len(stages):
                stages[t - s](s)
        if t >= len(stages) - 1:
            scan_step(t - (len(stages) - 1))

    o = o_s[0] + o_s[1]
    o = o * lax.rsqrt(jnp.mean(o * o, axis=-1, keepdims=True) + EPS) * ng_ref[...]
    gate = gate_ref[0]
    y_ref[0] = (o * (gate * jax.nn.sigmoid(gate))).astype(y_ref.dtype)


def _gdn(z3, g3, gt4, conv_w, norm_g):
    b, s, _ = z3.shape
    c = GDN_CHUNK
    n_chunks = s // c
    blk0 = 3 * W_A // HEAD_DIM
    head = lambda k: pl.BlockSpec((1, s, HEAD_DIM), lambda i, h, k=k: (i, 0, blk0 + k * GDN_HEADS + h))
    cw = lambda k: pl.BlockSpec((CONV_K, HEAD_DIM), lambda i, h, k=k: (0, k * GDN_HEADS + h))
    return pl.pallas_call(
        functools.partial(_gdn_body, seq=s),
        grid=(b, GDN_HEADS),
        in_specs=[
            head(0), head(1), head(2), head(3),
            pl.BlockSpec((1, s, LANE), lambda i, h: (i, 0, 0)),
            pl.BlockSpec((1, n_chunks, LANE, c), lambda i, h: (i, 0, 0, 0)),
            cw(0), cw(1), cw(2),
            pl.BlockSpec((1, HEAD_DIM), lambda i, h: (0, 0)),
            pl.BlockSpec((6, c, 2 * c), lambda i, h: (0, 0, 0)),
            pl.BlockSpec((4, c, 2 * c), lambda i, h: (0, 0, 0)),
        ],
        out_specs=pl.BlockSpec((1, s, HEAD_DIM), lambda i, h: (i, 0, h)),
        out_shape=jax.ShapeDtypeStruct((b, s, W_B), BF16),
        scratch_shapes=[
            pltpu.VMEM((n_chunks, c, HEAD_DIM), F32),
            pltpu.VMEM((n_chunks, c, HEAD_DIM), F32),
            pltpu.VMEM((n_chunks, c, HEAD_DIM), F32),
            pltpu.VMEM((n_chunks, c, c), F32),
            pltpu.VMEM((n_chunks, c, c), F32),
            pltpu.VMEM((n_chunks, c, 2 * c), F32),
            pltpu.VMEM((n_chunks, 2 * c, 2 * c), BF16),
            pltpu.VMEM((n_chunks, 3, c, 2 * c), BF16),
            pltpu.VMEM((n_chunks, c, 2 * c), BF16),
            pltpu.VMEM((n_chunks, 2, c, 2 * c), BF16),
            pltpu.VMEM((n_chunks, 2, c, HEAD_DIM), BF16),
            pltpu.VMEM((n_chunks, c, 2 * c), F32),
            pltpu.VMEM((n_chunks, 2, c, 2 * c), BF16),
            pltpu.VMEM((n_chunks, 2 * c, 2 * c), BF16),
            pltpu.VMEM((n_chunks, c, 2 * c), F32),
            pltpu.VMEM((n_chunks, c, 2 * c), F32),
            pltpu.VMEM((n_chunks, SUBLANE, 2 * c), F32),
            pltpu.VMEM((2, 2 * c, 2 * c), BF16),
            pltpu.VMEM((c, 2 * c), F32),
            pltpu.VMEM((2, s, HEAD_DIM), F32),
        ],
        compiler_params=_params("parallel", "arbitrary"),
        name="gdn",
    )(z3, z3, z3, z3, g3, gt4, conv_w, conv_w, conv_w, norm_g.reshape(1, HEAD_DIM), *_gdn_masks())


def _out_proj_body(ya_ref, yb_ref, yc_ref, wa_ref, wb_ref, wc_ref, x_ref, *rest, final_norm):
    acc = jnp.dot(ya_ref[...], wa_ref[...], preferred_element_type=F32)
    acc += jnp.dot(yb_ref[...], wb_ref[...], preferred_element_type=F32)
    acc += jnp.dot(yc_ref[...], wc_ref[...], preferred_element_type=F32)
    if not final_norm:
        (o_ref,) = rest
        o_ref[...] = x_ref[...] + acc
        return
    g_ref, o_ref, row_s = rest
    j = pl.program_id(1)
    n_col, _, tn = row_s.shape
    row_s[j] = x_ref[...] + acc

    @pl.when(j == n_col - 1)
    def _():
        ss = None
        for jj in range(n_col):
            part = jnp.sum(row_s[jj] * row_s[jj], axis=-1, keepdims=True)
            ss = part if ss is None else ss + part
        scale = lax.rsqrt(ss / (n_col * tn) + EPS)
        for jj in range(n_col):
            cols = slice(jj * tn, (jj + 1) * tn)
            o_ref[:, cols] = row_s[jj] * scale * g_ref[:, cols]


def _out_proj(ya, yb, yc, w_out, x, final_g=None, *, tm, tn):
    t, d = x.shape
    tm = min(tm, t)
    final_norm = final_g is not None
    row = lambda w: pl.BlockSpec((tm, w), lambda i, j: (i, 0))
    wrows = lambda w, first: pl.BlockSpec((w, tn), lambda i, j: (first // w, j))
    in_specs = [row(W_A), row(W_B), row(W_C),
                wrows(W_A, 0), wrows(W_B, W_A), wrows(W_C, W_A + W_B),
                pl.BlockSpec((tm, tn), lambda i, j: (i, j))]
    assert W_A % W_B == 0 and (W_A + W_B) % W_C == 0
    args = [ya, yb, yc, w_out, w_out, w_out, x]
    if final_norm:
        in_specs.append(pl.BlockSpec((1, d), lambda i, j: (0, 0)))
        args.append(final_g.reshape(1, d))
        out_spec = pl.BlockSpec((tm, d), lambda i, j: (i, 0))
        scratch = [pltpu.VMEM((d // tn, tm, tn), F32)]
    else:
        out_spec = pl.BlockSpec((tm, tn), lambda i, j: (i, j))
        scratch = []
    return pl.pallas_call(
        functools.partial(_out_proj_body, final_norm=final_norm),
        grid=(t // tm, d // tn),
        in_specs=in_specs,
        out_specs=out_spec,
        out_shape=jax.ShapeDtypeStruct((t, d), F32),
        scratch_shapes=scratch,
        compiler_params=_params("parallel", "arbitrary" if final_norm else "parallel"),
        name="out_proj_norm" if final_norm else "out_proj",
    )(*args)


def _lane_row(values, offset):
    flat = values.reshape(-1).astype(F32)
    return jnp.zeros((1, LANE), F32).at[0, offset:offset + flat.shape[0]].set(flat)


def _prep_w_in_body(w_ref, o_ref):
    w = w_ref[0]
    o_ref[:, :OFF_SMALL] = w[:, :OFF_SMALL]
    o_ref[:, OFF_SMALL:W_MAIN] = w[:, OFF_SMALL + N_SMALL:]
    small = w[:, OFF_SMALL:OFF_SMALL + N_SMALL]
    pad = jnp.zeros((small.shape[0], LANE - N_SMALL), small.dtype)
    o_ref[:, W_MAIN:] = jnp.concatenate([small, pad], axis=1)


def _prep_w_in(w_in, layer, *, rows):
    _, d, n_in = w_in.shape
    return pl.pallas_call(
        _prep_w_in_body,
        grid=(d // rows,),
        in_specs=[pl.BlockSpec((1, rows, n_in), lambda i: (layer, i, 0))],
        out_specs=pl.BlockSpec((rows, W_MAIN + LANE), lambda i: (i, 0)),
        out_shape=jax.ShapeDtypeStruct((d, W_MAIN + LANE), w_in.dtype),
        compiler_params=_params("parallel"),
        name="prep_w_in",
    )(w_in)


def _layer_weights(w_in_bf16, w_mem_kv, w_out, layer):
    return (_prep_w_in(w_in_bf16, layer, rows=512), w_mem_kv[layer].astype(BF16),
            w_out[layer].astype(BF16))


def _layer(x, mem, weights, norm_g, sgu_ln_g, sgu_ln_b, sgu_w, sgu_b, conv_w, a_log, dt_bias,
           gdn_norm_g, mem_norm_g, final_g):
    b, s, d = x.shape
    t = b * s
    x2 = x.reshape(t, d)
    w_all, w_kv, w_out = weights
    z, zs = _norm_matmul(x2, norm_g, w_all, W_MAIN // LANE, tm=512, tn=1280, out_dtype=F32)
    z3 = z.reshape(b, s, W_MAIN)

    g, gt = _gate_prep(zs, _lane_row(a_log, G_CUM), _lane_row(dt_bias, G_CUM), rows=512)
    g3 = g.reshape(b, s, LANE)
    gt4 = gt.reshape(b, s // GDN_CHUNK, LANE, GDN_CHUNK)

    ya = _sgu(z3, sgu_ln_g, sgu_ln_b, sgu_w, sgu_b, rows=512)
    yb = _gdn(z3, g3, gt4, conv_w, gdn_norm_g)
    kv = _norm_matmul(mem.reshape(-1, d), mem_norm_g, w_kv, tm=512, tn=1024, out_dtype=BF16)
    yc = _xattn(z3, kv.reshape(b, -1, 2 * W_C), rows=1024)

    if final_g is None:
        out = _out_proj(ya.reshape(t, W_A), yb.reshape(t, W_B), yc.reshape(t, W_C), w_out, x2,
                        tm=1024, tn=512)
    else:
        out = _out_proj(ya.reshape(t, W_A), yb.reshape(t, W_B), yc.reshape(t, W_C), w_out, x2,
                        final_g, tm=512, tn=512)
    return out.reshape(b, s, d)


def kernel(x_prompt, x_sample, mem_prompt, mem_sample, norm_g, w_in, sgu_ln_g, sgu_ln_b, sgu_w,
           sgu_b, conv_w, a_log, dt_bias, gdn_norm_g, mem_norm_g, w_mem_kv, w_out, final_g):
    depth = norm_g.shape[0]
    w_in_bf16 = w_in.astype(BF16)
    weights = [_layer_weights(w_in_bf16, w_mem_kv, w_out, l) for l in range(depth)]

    def trunk(x, mem):
        for l in range(depth):
            x = _layer(x, mem, weights[l], norm_g[l], sgu_ln_g[l], sgu_ln_b[l], sgu_w[l], sgu_b[l],
                       conv_w[l], a_log[l], dt_bias[l], gdn_norm_g[l], mem_norm_g[l],
                       final_g if l == depth - 1 else None)
        return x

    return trunk(x_prompt, mem_prompt), trunk(x_sample, mem_sample)
```

```python
import functools

import jax
import jax.numpy as jnp
import numpy as np
from jax import lax
from jax.experimental import pallas as pl
from jax.experimental.pallas import tpu as pltpu

F32 = jnp.float32
BF16 = jnp.bfloat16
EPS = 1e-6

SGU_CHUNK = 128
SGU_GROUPS = 12
GDN_HEADS = 12
HEAD_DIM = 128
CONV_K = 5
XA_HEADS = 4
XA_HEAD_DIM = 256
W_A = SGU_GROUPS * SGU_CHUNK
W_B = GDN_HEADS * HEAD_DIM
W_C = XA_HEADS * XA_HEAD_DIM
W_MAIN = 3 * W_A + 4 * W_B + 2 * W_C
N_SMALL = 4 * GDN_HEADS
OFF_SMALL = 3 * W_A + 4 * W_B

LANE = 128
SUBLANE = 8
VMEM_LIMIT = 56 * 1024 * 1024

GDN_CHUNK = 128
G_BETA, G_CUM, G_TOT = 0, 2 * GDN_HEADS, 4 * GDN_HEADS
M_BOUND, M_EYE, M_NEG_DIAG = range(3)
INV_BASE = 4
MERGE_SHIFTS = (2, 3, 4, 5, 6)
MASKED_OUT = -1e30


def _params(*sem):
    return pltpu.CompilerParams(dimension_semantics=sem, vmem_limit_bytes=VMEM_LIMIT)


def _norm_matmul_body(*refs, has_small):
    if has_small:
        x_ref, g_ref, w_ref, ws_ref, z_ref, zs_ref, h_ref = refs
    else:
        x_ref, g_ref, w_ref, z_ref, h_ref = refs

    @pl.when(pl.program_id(1) == 0)
    def _():
        x = x_ref[...]
        ms = jnp.mean(x * x, axis=-1, keepdims=True)
        h = (x * lax.rsqrt(ms + EPS) * g_ref[...]).astype(BF16)
        h_ref[...] = h
        if has_small:
            zs_ref[...] = jnp.dot(h, ws_ref[...], preferred_element_type=F32)

    z_ref[...] = jnp.dot(h_ref[...], w_ref[...], preferred_element_type=F32).astype(z_ref.dtype)


def _norm_matmul(x, g, w, small_block=None, *, tm, tn, out_dtype):
    t, d = x.shape
    has_small = small_block is not None
    n = small_block * LANE if has_small else w.shape[1]
    tm = min(tm, t)
    in_specs = [
        pl.BlockSpec((tm, d), lambda i, j: (i, 0)),
        pl.BlockSpec((1, d), lambda i, j: (0, 0)),
        pl.BlockSpec((d, tn), lambda i, j: (0, j)),
    ]
    out_shape = [jax.ShapeDtypeStruct((t, n), out_dtype)]
    out_specs = [pl.BlockSpec((tm, tn), lambda i, j: (i, j))]
    args = [x, g.reshape(1, d), w]
    if has_small:
        in_specs.append(pl.BlockSpec((d, LANE), lambda i, j: (0, small_block)))
        out_shape.append(jax.ShapeDtypeStruct((t, LANE), F32))
        out_specs.append(pl.BlockSpec((tm, LANE), lambda i, j: (i, 0)))
        args.append(w)
    outs = pl.pallas_call(
        functools.partial(_norm_matmul_body, has_small=has_small),
        grid=(t // tm, n // tn),
        in_specs=in_specs,
        out_specs=out_specs,
        out_shape=out_shape,
        scratch_shapes=[pltpu.VMEM((tm, d), BF16)],
        compiler_params=_params("parallel", "arbitrary"),
        name="norm_matmul_small" if has_small else "norm_matmul",
    )(*args)
    return outs if has_small else outs[0]


def _gate_prep_body(zs_ref, a_ref, dt_ref, g_ref, gt_ref, *, chunks):
    c = GDN_CHUNK
    lane = lax.broadcasted_iota(jnp.int32, (c, LANE), 1)
    ri = lax.broadcasted_iota(jnp.int32, (c, c), 0)
    ci = lax.broadcasted_iota(jnp.int32, (c, c), 1)
    lower = (ri >= ci).astype(F32)
    upper = (ri <= ci).astype(F32)
    is_decay = (lane >= G_CUM) & (lane < G_TOT)
    neg_a = -jnp.exp(a_ref[...])
    for n in range(chunks):
        zs = zs_ref[pl.ds(n * c, c), :]
        beta = jax.nn.sigmoid(zs)
        g = jnp.where(is_decay, neg_a * jax.nn.softplus(zs + dt_ref[...]), 0.0)
        cum_fw = jnp.dot(lower, g, preferred_element_type=F32, precision=lax.Precision.HIGHEST)
        cum_bw = jnp.dot(upper, g, preferred_element_type=F32, precision=lax.Precision.HIGHEST)
        tot = jnp.broadcast_to(jnp.sum(g, axis=0, keepdims=True), (c, LANE))
        tot = pltpu.roll(tot, G_TOT - G_CUM, 1)
        out = jnp.where(lane < G_CUM, beta,
                        jnp.where(lane < G_CUM + GDN_HEADS, cum_fw,
                                  jnp.where(lane < G_TOT, cum_bw,
                                            jnp.where(lane < G_TOT + 2 * GDN_HEADS, tot, 0.0))))
        g_ref[pl.ds(n * c, c), :] = out
        gt_ref[n] = out.T


def _gate_prep(zs, a_row, dt_row, *, rows):
    t = zs.shape[0]
    rows = min(rows, t)
    chunks = rows // GDN_CHUNK
    return pl.pallas_call(
        functools.partial(_gate_prep_body, chunks=chunks),
        grid=(t // rows,),
        in_specs=[
            pl.BlockSpec((rows, LANE), lambda i: (i, 0)),
            pl.BlockSpec((1, LANE), lambda i: (0, 0)),
            pl.BlockSpec((1, LANE), lambda i: (0, 0)),
        ],
        out_specs=[
            pl.BlockSpec((rows, LANE), lambda i: (i, 0)),
            pl.BlockSpec((chunks, LANE, GDN_CHUNK), lambda i: (i, 0, 0)),
        ],
        out_shape=[
            jax.ShapeDtypeStruct((t, LANE), F32),
            jax.ShapeDtypeStruct((t // GDN_CHUNK, LANE, GDN_CHUNK), F32),
        ],
        compiler_params=_params("parallel"),
        name="gate_prep",
    )(zs, a_row, dt_row)


def _sgu_body(u_ref, v_ref, gate_ref, lg_ref, lb_ref, ws_ref, bs_ref, y_ref, *, chunks):
    c = SGU_CHUNK
    v = jax.nn.gelu(v_ref[0])
    mu = jnp.mean(v, axis=-1, keepdims=True)
    vc = v - mu
    var = jnp.mean(vc * vc, axis=-1, keepdims=True)
    vn = (vc * lax.rsqrt(var + EPS) * lg_ref[...] + lb_ref[...]).astype(BF16)
    for n in range(chunks):
        for g in range(SGU_GROUPS):
            rows, cols = slice(n * c, (n + 1) * c), slice(g * c, (g + 1) * c)
            mixed = jnp.dot(ws_ref[g], vn[rows, cols], preferred_element_type=F32) + bs_ref[g]
            gate = gate_ref[0, rows, cols]
            y = jax.nn.gelu(u_ref[0, rows, cols]) * mixed * (gate * jax.nn.sigmoid(gate))
            y_ref[0, rows, cols] = y.astype(y_ref.dtype)


def _sgu(z3, ln_g, ln_b, w_s, b_s, *, rows):
    b, s, _ = z3.shape
    rows = min(rows, s)
    chunks = rows // SGU_CHUNK
    col = lambda k: pl.BlockSpec((1, rows, W_A), lambda i, r, k=k: (i, r, k))
    return pl.pallas_call(
        functools.partial(_sgu_body, chunks=chunks),
        grid=(b, s // rows),
        in_specs=[
            col(0), col(1), col(2),
            pl.BlockSpec((1, W_A), lambda i, r: (0, 0)),
            pl.BlockSpec((1, W_A), lambda i, r: (0, 0)),
            pl.BlockSpec((SGU_GROUPS, SGU_CHUNK, SGU_CHUNK), lambda i, r: (0, 0, 0)),
            pl.BlockSpec((SGU_GROUPS, SGU_CHUNK, LANE), lambda i, r: (0, 0, 0)),
        ],
        out_specs=pl.BlockSpec((1, rows, W_A), lambda i, r: (i, r, 0)),
        out_shape=jax.ShapeDtypeStruct((b, s, W_A), BF16),
        compiler_params=_params("parallel", "parallel"),
        name="sgu",
    )(z3, z3, z3, ln_g.reshape(1, W_A), ln_b.reshape(1, W_A), w_s.astype(BF16),
      jnp.broadcast_to(b_s[:, :, None], (SGU_GROUPS, SGU_CHUNK, LANE)))


XA_PAIR = 2 * XA_HEAD_DIM


def _xattn_body(q_ref, gate_ref, k_ref, v_ref, y_ref):
    for hh in range(XA_PAIR // XA_HEAD_DIM):
        cols = slice(hh * XA_HEAD_DIM, (hh + 1) * XA_HEAD_DIM)
        q = q_ref[0, :, cols].astype(BF16)
        s = lax.dot_general(q, k_ref[0, :, cols], (((1,), (1,)), ((), ())), preferred_element_type=F32)
        s = s * (XA_HEAD_DIM ** -0.5)
        s = s - jnp.max(s, axis=-1, keepdims=True)
        p = jnp.exp(s)
        p = p / jnp.sum(p, axis=-1, keepdims=True)
        o = jnp.dot(p.astype(BF16), v_ref[0, :, cols], preferred_element_type=F32)
        gate = gate_ref[0, :, cols]
        y_ref[0, :, cols] = (o * (gate * jax.nn.sigmoid(gate))).astype(y_ref.dtype)


def _xattn(z3, kv3, *, rows):
    b, s, _ = z3.shape
    rows = min(rows, s)
    n_mem = kv3.shape[1]
    pairs = W_C // XA_PAIR
    q_blk = (3 * W_A + 4 * W_B) // XA_PAIR
    gate_blk = q_blk + pairs
    return pl.pallas_call(
        _xattn_body,
        grid=(b, s // rows, pairs),
        in_specs=[
            pl.BlockSpec((1, rows, XA_PAIR), lambda i, r, h: (i, r, q_blk + h)),
            pl.BlockSpec((1, rows, XA_PAIR), lambda i, r, h: (i, r, gate_blk + h)),
            pl.BlockSpec((1, n_mem, XA_PAIR), lambda i, r, h: (i, 0, h)),
            pl.BlockSpec((1, n_mem, XA_PAIR), lambda i, r, h: (i, 0, pairs + h)),
        ],
        out_specs=pl.BlockSpec((1, rows, XA_PAIR), lambda i, r, h: (i, r, h)),
        out_shape=jax.ShapeDtypeStruct((b, s, W_C), BF16),
        compiler_params=_params("parallel", "parallel", "parallel"),
        name="xattn",
    )(z3, z3, kv3, kv3)


def _gdn_masks():
    c = GDN_CHUNK
    ri = np.arange(c)[:, None]
    ci = np.arange(c)[None, :]
    x = ri ^ ci
    halves = []
    for strict, incl in ((ri > ci, ri >= ci), (ri < ci, ri <= ci)):
        halves.append(np.stack([
            np.where(incl, 0.0, MASKED_OUT),
            ri == ci,
            -1.0 * (strict & (x < INV_BASE)),
        ] + [strict & ((x >> shift) == 1) for shift in MERGE_SHIFTS]).astype(np.float32))
    masks = np.concatenate(halves, axis=2)
    return jnp.asarray(masks[:M_NEG_DIAG + 1]), jnp.asarray(masks[M_NEG_DIAG:], dtype=BF16)


def _gdn_body(q_ref, k_ref, v_ref, gate_ref, g_ref, gt_ref, cq_ref, ck_ref, cv_ref, ng_ref, m_ref,
              mb_ref, y_ref,
              qn_s, kn_s, vn_s, qk0_s, kk_s,
              p_s, bd_s, off_s, qkd_s, rhs_s, kg_s, qg_s, uw_s,
              xq_s, c_s, op_s, eg_s, sbd_s, st_s, o_s, *, seq):
    c = GDN_CHUNK
    n_chunks = seq // c
    h = pl.program_id(1)
    lane = lax.broadcasted_iota(jnp.int32, (c, LANE), 1)
    lo, hi = slice(0, c), slice(c, 2 * c)
    halves = (lo, hi)

    bd_s[...] = jnp.zeros(bd_s.shape, BF16)
    sbd_s[...] = jnp.zeros(sbd_s.shape, BF16)
    st_s[...] = jnp.zeros(st_s.shape, F32)

    def conv_silu(x_ref, cw_ref, ci_):
        halo = SUBLANE
        c0 = ci_ * c
        left = (CONV_K - 1) // 2
        cw = cw_ref[...]
        if 0 < ci_ < n_chunks - 1:
            taps = [x_ref[0, c0 + j - left:c0 + j - left + c, :] for j in range(CONV_K)]
        else:
            zeros = jnp.zeros((halo, HEAD_DIM), F32)
            prev = x_ref[0, c0 - halo:c0, :] if ci_ > 0 else zeros
            nxt = x_ref[0, c0 + c:c0 + c + halo, :] if ci_ < n_chunks - 1 else zeros
            win = jnp.concatenate([prev, x_ref[0, c0:c0 + c, :], nxt], axis=0)
            taps = [win[halo + j - left:halo + j - left + c, :] for j in range(CONV_K)]
        acc = None
        for j in range(CONV_K):
            term = taps[j] * cw[j:j + 1, :]
            acc = term if acc is None else acc + term
        return acc * jax.nn.sigmoid(acc)

    def l2n(x, scale=1.0):
        return x * (lax.rsqrt(jnp.sum(x * x, axis=-1, keepdims=True) + EPS) * scale)

    def column(gc, idx):
        return jnp.sum(jnp.where(lane == idx, gc, 0.0), axis=1, keepdims=True)

    def gt_row(ci_, idx):
        return gt_ref[0, ci_, pl.ds(idx, 1), :]

    def set_bd(buf, j, x):
        buf[j, lo, lo] = x[:, lo]
        buf[j, hi, hi] = x[:, hi]

    def packed_lhs(buf, j):
        return jnp.concatenate([buf[j, lo, lo], buf[j, hi, hi]], axis=1)

    cached = set()

    def chunk_inputs(ci_):
        if ci_ in cached:
            return qn_s[ci_], kn_s[ci_], vn_s[ci_], qk0_s[ci_], kk_s[ci_]
        cached.add(ci_)
        q = l2n(conv_silu(q_ref, cq_ref, ci_), HEAD_DIM ** -0.5)
        k = l2n(conv_silu(k_ref, ck_ref, ci_))
        v = conv_silu(v_ref, cv_ref, ci_)
        kb = k.astype(BF16)
        qkk = lax.dot_general(jnp.concatenate([q.astype(BF16), kb], axis=0), kb,
                              (((1,), (1,)), ((), ())), preferred_element_type=F32)
        qn_s[ci_], kn_s[ci_], vn_s[ci_] = q, k, v
        qk0_s[ci_], kk_s[ci_] = qkk[:c], qkk[c:]
        return q, k, v, qkk[:c], qkk[c:]

    def build(s):
        kd, qkd = [], []
        for d in range(2):
            ci_ = s if d == 0 else n_chunks - 1 - s
            q, k, v, qk0, kk = chunk_inputs(ci_)
            gc = g_ref[0, ci_ * c:(ci_ + 1) * c, :]
            beta = column(gc, G_BETA + GDN_HEADS * d + h)
            cum_c = column(gc, G_CUM + GDN_HEADS * d + h)
            cum_r = gt_row(ci_, G_CUM + GDN_HEADS * d + h)
            tot_r = gt_row(ci_, G_TOT + GDN_HEADS * d + h)
            e = jnp.exp(jnp.minimum(cum_c - cum_r, m_ref[M_BOUND, :, halves[d]]))
            e_c = jnp.exp(cum_c)
            kd.append(kk * beta * e)
            qkd.append(qk0 * e)
            rhs_s[s, d] = jnp.concatenate([v * beta, k * (beta * e_c)], axis=1).astype(BF16)
            kg_s[s, d] = (k * jnp.exp(tot_r - cum_c)).astype(BF16)
            qg_s[s, :, halves[d]] = q * e_c
            eg_s[s, :, halves[d]] = jnp.broadcast_to(jnp.exp(tot_r), (SUBLANE, c))
        kd = jnp.concatenate(kd, axis=1)
        qkd_s[s] = jnp.concatenate(qkd, axis=1).astype(BF16)
        p_s[s] = m_ref[M_EYE] + kd * m_ref[M_NEG_DIAG]
        kd = kd.astype(BF16)
        set_bd(bd_s, s, kd * mb_ref[0])
        for lvl in range(len(MERGE_SHIFTS)):
            off_s[s, lvl] = kd * mb_ref[1 + lvl]

    def neumann_first(j):
        n2 = jnp.dot(packed_lhs(bd_s, j), bd_s[j], preferred_element_type=F32)
        set_bd(bd_s, j, n2.astype(BF16))

    def neumann_step(j, last):
        n = packed_lhs(bd_s, j)
        rhs = bd_s[j]
        p = p_s[j]
        p_s[j] = p + jnp.dot(p.astype(BF16), rhs, preferred_element_type=F32)
        if not last:
            set_bd(bd_s, j, jnp.dot(n, rhs, preferred_element_type=F32).astype(BF16))

    def merge_a(j, lvl):
        set_bd(bd_s, j, p_s[j].astype(BF16))
        x = jnp.dot(off_s[j, lvl], bd_s[j], preferred_element_type=F32)
        set_bd(bd_s, j, x.astype(BF16))

    def merge_b(j):
        t = p_s[j]
        p_s[j] = t - jnp.dot(t.astype(BF16), bd_s[j], preferred_element_type=F32)

    def apply_t(j):
        t = p_s[j].astype(BF16)
        for d in range(2):
            uw = jnp.dot(t[:, halves[d]], rhs_s[j, d], preferred_element_type=F32)
            uw_s[j, d] = uw.astype(BF16)

    def finish(j):
        for d in range(2):
            uw = uw_s[j, d]
            cx = lax.dot_general(kg_s[j, d], uw, (((0,), (0,)), ((), ())), preferred_element_type=F32)
            ow = jnp.dot(qkd_s[j, :, halves[d]], uw, preferred_element_type=F32)
            c_s[j, :, halves[d]] = cx[:, lo]
            op_s[j, :, halves[d]] = ow[:, lo]
            xq_s[j, lo, halves[d]] = cx[:, hi].astype(BF16)
            xq_s[j, hi, halves[d]] = (qg_s[j, :, halves[d]] - ow[:, hi]).astype(BF16)

    stages = [
        build,
        neumann_first,
        functools.partial(neumann_step, last=True),
    ]
    for lvl in range(len(MERGE_SHIFTS)):
        stages += [functools.partial(merge_a, lvl=lvl), merge_b]
    stages += [apply_t, finish]

    def scan_step(s):
        buf = s % 2
        state = st_s[...]
        set_bd(sbd_s, buf, state.astype(BF16))
        rhs = sbd_s[buf]
        xs = jnp.dot(xq_s[s, lo, :], rhs, preferred_element_type=F32)
        out = jnp.dot(xq_s[s, hi, :], rhs, preferred_element_type=F32) + op_s[s]
        o_s[0, s * c:(s + 1) * c, :] = out[:, lo]
        o_s[1, (n_chunks - 1 - s) * c:(n_chunks - s) * c, :] = out[:, hi]
        st_s[...] = state * eg_s[s, 0:1, :] - xs + c_s[s]

    for t in range(n_chunks + len(stages) - 1):
        for s in range(n_chunks):
            if 0 <= t - s < len(stages):
                stages[t - s](s)
        if t >= len(stages) - 1:
            scan_step(t - (len(stages) - 1))

    o = o_s[0] + o_s[1]
    o = o * lax.rsqrt(jnp.mean(o * o, axis=-1, keepdims=True) + EPS) * ng_ref[...]
    gate = gate_ref[0]
    y_ref[0] = (o * (gate * jax.nn.sigmoid(gate))).astype(y_ref.dtype)


def _gdn(z3, g3, gt4, conv_w, norm_g):
    b, s, _ = z3.shape
    c = GDN_CHUNK
    n_chunks = s // c
    blk0 = 3 * W_A // HEAD_DIM
    head = lambda k: pl.BlockSpec((1, s, HEAD_DIM), lambda i, h, k=k: (i, 0, blk0 + k * GDN_HEADS + h))
    cw = lambda k: pl.BlockSpec((CONV_K, HEAD_DIM), lambda i, h, k=k: (0, k * GDN_HEADS + h))
    return pl.pallas_call(
        functools.partial(_gdn_body, seq=s),
        grid=(b, GDN_HEADS),
        in_specs=[
            head(0), head(1), head(2), head(3),
            pl.BlockSpec((1, s, LANE), lambda i, h: (i, 0, 0)),
            pl.BlockSpec((1, n_chunks, LANE, c), lambda i, h: (i, 0, 0, 0)),
            cw(0), cw(1), cw(2),
            pl.BlockSpec((1, HEAD_DIM), lambda i, h: (0, 0)),
            pl.BlockSpec((M_NEG_DIAG + 1, c, 2 * c), lambda i, h: (0, 0, 0)),
            pl.BlockSpec((1 + len(MERGE_SHIFTS), c, 2 * c), lambda i, h: (0, 0, 0)),
        ],
        out_specs=pl.BlockSpec((1, s, HEAD_DIM), lambda i, h: (i, 0, h)),
        out_shape=jax.ShapeDtypeStruct((b, s, W_B), BF16),
        scratch_shapes=[
            pltpu.VMEM((n_chunks, c, HEAD_DIM), F32),
            pltpu.VMEM((n_chunks, c, HEAD_DIM), F32),
            pltpu.VMEM((n_chunks, c, HEAD_DIM), F32),
            pltpu.VMEM((n_chunks, c, c), F32),
            pltpu.VMEM((n_chunks, c, c), F32),
            pltpu.VMEM((n_chunks, c, 2 * c), F32),
            pltpu.VMEM((n_chunks, 2 * c, 2 * c), BF16),
            pltpu.VMEM((n_chunks, len(MERGE_SHIFTS), c, 2 * c), BF16),
            pltpu.VMEM((n_chunks, c, 2 * c), BF16),
            pltpu.VMEM((n_chunks, 2, c, 2 * c), BF16),
            pltpu.VMEM((n_chunks, 2, c, HEAD_DIM), BF16),
            pltpu.VMEM((n_chunks, c, 2 * c), F32),
            pltpu.VMEM((n_chunks, 2, c, 2 * c), BF16),
            pltpu.VMEM((n_chunks, 2 * c, 2 * c), BF16),
            pltpu.VMEM((n_chunks, c, 2 * c), F32),
            pltpu.VMEM((n_chunks, c, 2 * c), F32),
            pltpu.VMEM((n_chunks, SUBLANE, 2 * c), F32),
            pltpu.VMEM((2, 2 * c, 2 * c), BF16),
            pltpu.VMEM((c, 2 * c), F32),
            pltpu.VMEM((2, s, HEAD_DIM), F32),
        ],
        compiler_params=_params("parallel", "arbitrary"),
        name="gdn",
    )(z3, z3, z3, z3, g3, gt4, conv_w, conv_w, conv_w, norm_g.reshape(1, HEAD_DIM), *_gdn_masks())


def _out_proj_body(ya_ref, yb_ref, yc_ref, wa_ref, wb_ref, wc_ref, x_ref, *rest, final_norm):
    acc = jnp.dot(ya_ref[...], wa_ref[...], preferred_element_type=F32)
    acc += jnp.dot(yb_ref[...], wb_ref[...], preferred_element_type=F32)
    acc += jnp.dot(yc_ref[...], wc_ref[...], preferred_element_type=F32)
    if not final_norm:
        (o_ref,) = rest
        o_ref[...] = x_ref[...] + acc
        return
    g_ref, o_ref, row_s = rest
    j = pl.program_id(1)
    n_col, _, tn = row_s.shape
    row_s[j] = x_ref[...] + acc

    @pl.when(j == n_col - 1)
    def _():
        ss = None
        for jj in range(n_col):
            part = jnp.sum(row_s[jj] * row_s[jj], axis=-1, keepdims=True)
            ss = part if ss is None else ss + part
        scale = lax.rsqrt(ss / (n_col * tn) + EPS)
        for jj in range(n_col):
            cols = slice(jj * tn, (jj + 1) * tn)
            o_ref[:, cols] = row_s[jj] * scale * g_ref[:, cols]


def _out_proj(ya, yb, yc, w_out, x, final_g=None, *, tm, tn):
    t, d = x.shape
    tm = min(tm, t)
    final_norm = final_g is not None
    row = lambda w: pl.BlockSpec((tm, w), lambda i, j: (i, 0))
    wrows = lambda w, first: pl.BlockSpec((w, tn), lambda i, j: (first // w, j))
    in_specs = [row(W_A), row(W_B), row(W_C),
                wrows(W_A, 0), wrows(W_B, W_A), wrows(W_C, W_A + W_B),
                pl.BlockSpec((tm, tn), lambda i, j: (i, j))]
    assert W_A % W_B == 0 and (W_A + W_B) % W_C == 0
    args = [ya, yb, yc, w_out, w_out, w_out, x]
    if final_norm:
        in_specs.append(pl.BlockSpec((1, d), lambda i, j: (0, 0)))
        args.append(final_g.reshape(1, d))
        out_spec = pl.BlockSpec((tm, d), lambda i, j: (i, 0))
        scratch = [pltpu.VMEM((d // tn, tm, tn), F32)]
    else:
        out_spec = pl.BlockSpec((tm, tn), lambda i, j: (i, j))
        scratch = []
    return pl.pallas_call(
        functools.partial(_out_proj_body, final_norm=final_norm),
        grid=(t // tm, d // tn),
        in_specs=in_specs,
        out_specs=out_spec,
        out_shape=jax.ShapeDtypeStruct((t, d), F32),
        scratch_shapes=scratch,
        compiler_params=_params("parallel", "arbitrary" if final_norm else "parallel"),
        name="out_proj_norm" if final_norm else "out_proj",
    )(*args)


def _lane_row(values, offset):
    flat = values.reshape(-1).astype(F32)
    return jnp.zeros((1, LANE), F32).at[0, offset:offset + flat.shape[0]].set(flat)


def _prep_w_in_body(w_ref, o_ref):
    w = w_ref[0]
    o_ref[:, :OFF_SMALL] = w[:, :OFF_SMALL]
    o_ref[:, OFF_SMALL:W_MAIN] = w[:, OFF_SMALL + N_SMALL:]
    small = w[:, OFF_SMALL:OFF_SMALL + N_SMALL]
    pad = jnp.zeros((small.shape[0], LANE - N_SMALL), small.dtype)
    o_ref[:, W_MAIN:] = jnp.concatenate([small, pad], axis=1)


def _prep_w_in(w_in, layer, *, rows):
    _, d, n_in = w_in.shape
    return pl.pallas_call(
        _prep_w_in_body,
        grid=(d // rows,),
        in_specs=[pl.BlockSpec((1, rows, n_in), lambda i: (layer, i, 0))],
        out_specs=pl.BlockSpec((rows, W_MAIN + LANE), lambda i: (i, 0)),
        out_shape=jax.ShapeDtypeStruct((d, W_MAIN + LANE), w_in.dtype),
        compiler_params=_params("parallel"),
        name="prep_w_in",
    )(w_in)


def _layer_weights(w_in_bf16, w_mem_kv, w_out, layer):
    return (_prep_w_in(w_in_bf16, layer, rows=512), w_mem_kv[layer].astype(BF16),
            w_out[layer].astype(BF16))


def _layer(x, mem, weights, norm_g, sgu_ln_g, sgu_ln_b, sgu_w, sgu_b, conv_w, a_log, dt_bias,
           gdn_norm_g, mem_norm_g, final_g):
    b, s, d = x.shape
    t = b * s
    x2 = x.reshape(t, d)
    w_all, w_kv, w_out = weights
    z, zs = _norm_matmul(x2, norm_g, w_all, W_MAIN // LANE, tm=512, tn=1280, out_dtype=F32)
    z3 = z.reshape(b, s, W_MAIN)

    g, gt = _gate_prep(zs, _lane_row(a_log, G_CUM), _lane_row(dt_bias, G_CUM), rows=512)
    g3 = g.reshape(b, s, LANE)
    gt4 = gt.reshape(b, s // GDN_CHUNK, LANE, GDN_CHUNK)

    ya = _sgu(z3, sgu_ln_g, sgu_ln_b, sgu_w, sgu_b, rows=512)
    yb = _gdn(z3, g3, gt4, conv_w, gdn_norm_g)
    kv = _norm_matmul(mem.reshape(-1, d), mem_norm_g, w_kv, tm=512, tn=1024, out_dtype=BF16)
    yc = _xattn(z3, kv.reshape(b, -1, 2 * W_C), rows=1024)

    if final_g is None:
        out = _out_proj(ya.reshape(t, W_A), yb.reshape(t, W_B), yc.reshape(t, W_C), w_out, x2,
                        tm=1024, tn=512)
    else:
        out = _out_proj(ya.reshape(t, W_A), yb.reshape(t, W_B), yc.reshape(t, W_C), w_out, x2,
                        final_g, tm=512, tn=512)
    return out.reshape(b, s, d)


def kernel(x_prompt, x_sample, mem_prompt, mem_sample, norm_g, w_in, sgu_ln_g, sgu_ln_b, sgu_w,
           sgu_b, conv_w, a_log, dt_bias, gdn_norm_g, mem_norm_g, w_mem_kv, w_out, final_g):
    depth = norm_g.shape[0]
    w_in_bf16 = w_in.astype(BF16)
    weights = [_layer_weights(w_in_bf16, w_mem_kv, w_out, l) for l in range(depth)]

    def trunk(x, mem):
        for l in range(depth):
            x = _layer(x, mem, weights[l], norm_g[l], sgu_ln_g[l], sgu_ln_b[l], sgu_w[l], sgu_b[l],
                       conv_w[l], a_log[l], dt_bias[l], gdn_norm_g[l], mem_norm_g[l],
                       final_g if l == depth - 1 else None)
        return x

    return trunk(x_prompt, mem_prompt), trunk(x_sample, mem_sample)
```
